```python
import math
import jax, jax.numpy as jnp
from jax import lax
import numpy as np

D_MODEL = 1024
BATCH = 2
SEQ = 8192
DEPTH = 1

CHUNK = 64
N_META = 16
N_HEADS = 8
HEAD_DIM = 64
V_HEAD_DIM = 2 * HEAD_DIM
ATT_QK = N_HEADS * 2 * HEAD_DIM
ATT_V = N_HEADS * V_HEAD_DIM
CONV_CH = D_MODEL
CONV_K = 31
D_FF = 2816
FFN_CONV_K = 3
QB = 128
NEG_INF = -1e30
IN_COLS = 2 * ATT_QK + ATT_V + 2 * CONV_CH + 2 * D_MODEL

kernel_name = "hybrid_diffattn_conformer_conv_stream_block"


def _rms(x, w, eps=1e-6):
    xf = x.astype(jnp.float32)
    y = xf * lax.rsqrt(jnp.mean(xf * xf, axis=-1, keepdims=True) + eps)
    return (y * w.astype(jnp.float32)).astype(x.dtype)


def _layernorm(x, w, b, eps=1e-5):
    xf = x.astype(jnp.float32)
    mu = jnp.mean(xf, axis=-1, keepdims=True)
    xc = xf - mu
    y = xc * lax.rsqrt(jnp.mean(xc * xc, axis=-1, keepdims=True) + eps)
    return (y * w.astype(jnp.float32) + b.astype(jnp.float32)).astype(x.dtype)


def _causal_dwconv(x, w, b):
    k, c = w.shape
    y = lax.conv_general_dilated(
        x, w[:, None, :].astype(x.dtype), window_strides=(1,), padding=[(k - 1, 0)],
        dimension_numbers=('NWC', 'WIO', 'NWC'), feature_group_count=c)
    return y + b.astype(x.dtype)


def _diff_attention(q, k, v, lam, cid):
    b, lp = q.shape[0], q.shape[1]
    nblk = lp // QB
    qf = q.astype(jnp.float32) * (HEAD_DIM ** -0.5)
    kf = k.astype(jnp.float32)
    vf = v.astype(jnp.float32)

    def block(i):
        start = i * QB
        qb = lax.dynamic_slice_in_dim(qf, start, QB, axis=1)
        qc = lax.dynamic_slice_in_dim(cid, start, QB, axis=0)
        s = jnp.einsum('bqhmd,bkhmd->bhmqk', qb, kf)
        mask = cid[None, :] <= qc[:, None]
        p = jax.nn.softmax(jnp.where(mask, s, NEG_INF), axis=-1)
        a = p[:, :, 0] - lam * p[:, :, 1]
        return jnp.einsum('bhqk,bkhe->bqhe', a, vf)

    o = lax.map(block, jnp.arange(nblk))
    return jnp.moveaxis(o, 0, 1).reshape(b, lp, N_HEADS, V_HEAD_DIM)


def setup_inputs(seed: int = 0) -> dict:
    key = jax.random.key(seed)
    ks = jax.random.split(key, 24)
    f32 = jnp.float32
    nrm = lambda k, shape, scale: (jax.random.normal(k, shape, f32) * scale)
    gain = lambda k, shape: 1.0 + 0.05 * jax.random.normal(k, shape, f32)
    L = DEPTH
    return {
        "x": nrm(ks[0], (BATCH, SEQ, D_MODEL), 1.0),
        "meta_tokens": nrm(ks[1], (N_META, D_MODEL), 0.5),
        "norm_mix_w": gain(ks[2], (L, D_MODEL)),
        "w_in": nrm(ks[3], (L, D_MODEL, IN_COLS), D_MODEL ** -0.5),
        "lambda_q1": nrm(ks[4], (L, HEAD_DIM), 0.1),
        "lambda_k1": nrm(ks[5], (L, HEAD_DIM), 0.1),
        "lambda_q2": nrm(ks[6], (L, HEAD_DIM), 0.1),
        "lambda_k2": nrm(ks[7], (L, HEAD_DIM), 0.1),
        "subln_w": gain(ks[8], (L, V_HEAD_DIM)),
        "conv_dw_w": nrm(ks[9], (L, CONV_K, CONV_CH), CONV_K ** -0.5),
        "conv_dw_b": nrm(ks[10], (L, CONV_CH), 0.02),
        "conv_ln_w": gain(ks[11], (L, CONV_CH)),
        "conv_ln_b": nrm(ks[12], (L, CONV_CH), 0.02),
        "w_conv_out": nrm(ks[13], (L, CONV_CH, D_MODEL), CONV_CH ** -0.5),
        "w_out": nrm(ks[14], (L, D_MODEL, D_MODEL), D_MODEL ** -0.5),
        "norm_ffn_w": gain(ks[15], (L, D_MODEL)),
        "w_up": nrm(ks[16], (L, D_MODEL, 2 * D_FF), D_MODEL ** -0.5),
        "ffn_dw_w": nrm(ks[17], (L, FFN_CONV_K, 2 * D_FF), FFN_CONV_K ** -0.5),
        "ffn_dw_b": nrm(ks[18], (L, 2 * D_FF), 0.02),
        "w_down": nrm(ks[19], (L, D_FF, D_MODEL), D_FF ** -0.5),
        "norm_final_w": gain(ks[20], (D_MODEL,)),
    }


def reference(x, meta_tokens, norm_mix_w, w_in, lambda_q1, lambda_k1, lambda_q2, lambda_k2,
              subln_w, conv_dw_w, conv_dw_b, conv_ln_w, conv_ln_b, w_conv_out, w_out,
              norm_ffn_w, w_up, ffn_dw_w, ffn_dw_b, w_down, norm_final_w):
    b, s, d = x.shape
    total = s + N_META
    lp = ((total + QB - 1) // QB) * QB
    meta = jnp.broadcast_to(meta_tokens[None].astype(x.dtype), (b, N_META, d))
    h = jnp.concatenate([meta, x], axis=1)
    h = jnp.pad(h, ((0, 0), (0, lp - total), (0, 0)))
    pos = jnp.arange(lp)
    cid = jnp.where(pos < N_META, 0, (pos - N_META) // CHUNK + 1)

    splits = [ATT_QK, 2 * ATT_QK, 2 * ATT_QK + ATT_V,
              2 * ATT_QK + ATT_V + 2 * CONV_CH, 2 * ATT_QK + ATT_V + 2 * CONV_CH + D_MODEL]
    for l in range(DEPTH):
        lambda_init = 0.8 - 0.6 * math.exp(-0.3 * l)
        n = _rms(h, norm_mix_w[l])
        z = n @ w_in[l]
        q, k, v, glu, g_att, g_conv = jnp.split(z, splits, axis=-1)
        lam = (jnp.exp(jnp.sum(lambda_q1[l].astype(jnp.float32) * lambda_k1[l].astype(jnp.float32)))
               - jnp.exp(jnp.sum(lambda_q2[l].astype(jnp.float32) * lambda_k2[l].astype(jnp.float32)))
               + lambda_init)
        q = q.reshape(b, lp, N_HEADS, 2, HEAD_DIM)
        k = k.reshape(b, lp, N_HEADS, 2, HEAD_DIM)
        v = v.reshape(b, lp, N_HEADS, V_HEAD_DIM)
        o = _diff_attention(q, k, v, lam, cid)
        y_att = (_rms(o, subln_w[l]) * (1.0 - lambda_init)).reshape(b, lp, ATT_V).astype(h.dtype)
        u = glu[..., :CONV_CH] * jax.nn.sigmoid(glu[..., CONV_CH:])
        u = _causal_dwconv(u, conv_dw_w[l], conv_dw_b[l])
        u = jax.nn.silu(_layernorm(u, conv_ln_w[l], conv_ln_b[l]))
        y_conv = u @ w_conv_out[l]
        m = jax.nn.sigmoid(g_att) * y_att + jax.nn.sigmoid(g_conv) * y_conv
        h = h + m @ w_out[l]
        n2 = _rms(h, norm_ffn_w[l])
        up = _causal_dwconv(n2 @ w_up[l], ffn_dw_w[l], ffn_dw_b[l])
        gate, val = jnp.split(up, 2, axis=-1)
        h = h + (jax.nn.silu(gate) * val) @ w_down[l]

    out = _rms(h, norm_final_w)
    return out[:, N_META:N_META + s]
```

```python
import functools
import math

import jax
import jax.numpy as jnp
from jax import lax
from jax.experimental import pallas as pl
from jax.experimental.pallas import tpu as pltpu

F32 = jnp.float32
BF16 = jnp.bfloat16

D_MODEL = 1024
N_META = 16
CHUNK = 64
N_HEADS = 8
HEAD_DIM = 64
V_HEAD_DIM = 2 * HEAD_DIM
CONV_K = 31
D_FF = 2816
FFN_CONV_K = 3
NEG_INF = -1e30
LAMBDA_INIT = 0.8 - 0.6 * math.exp(0.0)

FRONT = 256
HALO = 32
FFN_HALO = 8
CONV_ROWS = 32

TM_IN = 768
TM_MIX = 384
TM_FFN = 256
TQ = 256
TK = 256
VMEM_LIMIT = 56 * 1024 * 1024


def _dot(a, b):
    return jnp.dot(a, b, preferred_element_type=F32)


def _sigmoid(x):
    return 1.0 / (1.0 + jnp.exp(-x))


def _rms_rows(x, w, eps=1e-6):
    return x * lax.rsqrt(jnp.mean(x * x, axis=-1, keepdims=True) + eps) * w


def _params(n_axes):
    return pltpu.CompilerParams(
        dimension_semantics=("arbitrary",) * n_axes, vmem_limit_bytes=VMEM_LIMIT)


def _const_spec(shape):
    return pl.BlockSpec(shape, lambda *_: (0,) * len(shape))


def _in_proj_kernel(h_ref, nw_ref, w_ref, q_ref, k_ref, v_ref, u_ref):
    d = D_MODEL
    n = _rms_rows(h_ref[...], nw_ref[...]).astype(BF16)
    q_ref[...] = (_dot(n, w_ref[:, 0:d]) * (HEAD_DIM ** -0.5)).astype(BF16)
    k_ref[...] = _dot(n, w_ref[:, d:2 * d]).astype(BF16)
    v_ref[...] = _dot(n, w_ref[:, 2 * d:3 * d]).astype(BF16)
    a = _dot(n, w_ref[:, 3 * d:4 * d])
    g = _dot(n, w_ref[:, 4 * d:5 * d])
    u_ref[...] = a * _sigmoid(g)


def _in_proj(hp, norm_w, w_qkvu):
    b, l, d = hp.shape
    tm = TM_IN
    row = pl.BlockSpec((None, tm, d), lambda bi, i: (bi, i, 0))
    return pl.pallas_call(
        _in_proj_kernel,
        grid=(b, l // tm),
        in_specs=[row, _const_spec((1, d)), _const_spec(w_qkvu.shape)],
        out_specs=[row, row, row, row],
        out_shape=[jax.ShapeDtypeStruct((b, l, d), BF16)] * 3
        + [jax.ShapeDtypeStruct((b, l, d), F32)],
        compiler_params=_params(2),
        name="in_proj",
    )(hp, norm_w, w_qkvu)


def _attn_kernel(q_ref, k_ref, v_ref, lq1_ref, lk1_ref, lq2_ref, lk2_ref, sw_ref, o_ref,
                 qs_ref, m_ref, l_ref, acc_ref):
    i = pl.program_id(2)
    tq, tk = TQ, TK

    q = q_ref[...]
    lane = lax.broadcasted_iota(jnp.int32, q.shape, 1)
    zero = jnp.zeros_like(q)
    qs_ref[0:tq, :] = jnp.where(lane < HEAD_DIM, q, zero)
    qs_ref[tq:, :] = jnp.where(lane >= HEAD_DIM, q, zero)
    m_ref[...] = jnp.full(m_ref.shape, NEG_INF, F32)
    l_ref[...] = jnp.zeros(l_ref.shape, F32)
    acc_ref[...] = jnp.zeros(acc_ref.shape, F32)

    def step(j, mask):
        start = pl.multiple_of(j * tk, tk)
        k = k_ref[pl.ds(start, tk), :]
        v = v_ref[pl.ds(start, tk), :]
        s = lax.dot_general(qs_ref[...], k, (((1,), (1,)), ((), ())),
                            preferred_element_type=F32)
        if mask is not None:
            s = jnp.where(mask, s, NEG_INF)
        m_prev = m_ref[...]
        m_new = jnp.maximum(m_prev, jnp.max(s, axis=1, keepdims=True))
        alpha = jnp.exp(m_prev - m_new)
        p = jnp.exp(s - m_new)
        l_ref[...] = alpha * l_ref[...] + jnp.sum(p, axis=1, keepdims=True)
        acc_ref[...] = alpha * acc_ref[...] + _dot(p.astype(BF16), v)
        m_ref[...] = m_new

    row = lax.broadcasted_iota(jnp.int32, (2 * tq, tk), 0) & (tq - 1)
    col = lax.broadcasted_iota(jnp.int32, (2 * tq, tk), 1)
    step(0, col >= FRONT - N_META)

    def body(j, carry):
        step(j, None)
        return carry

    lax.fori_loop(1, i, body, 0)

    @pl.when(i > 0)
    def _():
        step(i, (col // CHUNK) <= (row // CHUNK))

    lam = (jnp.exp(jnp.sum(lq1_ref[...] * lk1_ref[...], axis=-1, keepdims=True))
           - jnp.exp(jnp.sum(lq2_ref[...] * lk2_ref[...], axis=-1, keepdims=True))
           + LAMBDA_INIT)
    o0 = acc_ref[0:tq, :] / l_ref[0:tq, :]
    o1 = acc_ref[tq:, :] / l_ref[tq:, :]
    o = o0 - lam * o1
    y = _rms_rows(o, sw_ref[...]) * (1.0 - LAMBDA_INIT)
    o_ref[...] = y.astype(BF16)


def _attention(q, k, v, lq1, lk1, lq2, lk2, subln_w):
    b, l, d = q.shape
    tq = TQ
    qspec = pl.BlockSpec((None, tq, V_HEAD_DIM), lambda bi, h, i: (bi, i, h))
    kvspec = pl.BlockSpec((None, l, V_HEAD_DIM), lambda bi, h, i: (bi, 0, h))
    lspec = _const_spec((1, HEAD_DIM))
    return pl.pallas_call(
        _attn_kernel,
        grid=(b, N_HEADS, l // tq),
        in_specs=[qspec, kvspec, kvspec, lspec, lspec, lspec, lspec,
                  _const_spec((1, V_HEAD_DIM))],
        out_specs=qspec,
        out_shape=jax.ShapeDtypeStruct((b, l, d), BF16),
        scratch_shapes=[
            pltpu.VMEM((2 * tq, V_HEAD_DIM), BF16),
            pltpu.VMEM((2 * tq, 1), F32),
            pltpu.VMEM((2 * tq, 1), F32),
            pltpu.VMEM((2 * tq, V_HEAD_DIM), F32),
        ],
        compiler_params=_params(3),
        name="diff_attn",
    )(q, k, v, lq1, lk1, lq2, lk2, subln_w)


def _mix_kernel(u_ref, uh_ref, h_ref, ya_ref, cw_ref, cb_ref, lnw_ref, lnb_ref, wco_ref,
                nw_ref, wg_ref, wo_ref, o_ref, ubuf_ref, act_ref):
    i = pl.program_id(1)
    tm, d = act_ref.shape

    halo = uh_ref[...]
    ubuf_ref[0:HALO, :] = jnp.where(i > 0, halo, jnp.zeros_like(halo))
    ubuf_ref[HALO:, :] = u_ref[...]

    def conv_block(rb, carry):
        base = pl.multiple_of(rb * CONV_ROWS, CONV_ROWS)
        win = ubuf_ref[pl.ds(base, CONV_ROWS + HALO), :]
        acc = jnp.broadcast_to(cb_ref[...], (CONV_ROWS, d))
        for t in range(CONV_K):
            r0 = HALO - CONV_K + 1 + t
            acc = acc + cw_ref[t:t + 1, :] * win[r0:r0 + CONV_ROWS, :]
        mu = jnp.mean(acc, axis=-1, keepdims=True)
        xc = acc - mu
        y = xc * lax.rsqrt(jnp.mean(xc * xc, axis=-1, keepdims=True) + 1e-5)
        y = y * lnw_ref[...] + lnb_ref[...]
        act_ref[pl.ds(base, CONV_ROWS), :] = (y * _sigmoid(y)).astype(BF16)
        return carry

    lax.fori_loop(0, tm // CONV_ROWS, conv_block, 0)

    y_conv = _dot(act_ref[...], wco_ref[...])
    h = h_ref[...]
    n = _rms_rows(h, nw_ref[...]).astype(BF16)
    g_att = _dot(n, wg_ref[:, 0:d])
    g_conv = _dot(n, wg_ref[:, d:2 * d])
    m = _sigmoid(g_att) * ya_ref[...].astype(F32) + _sigmoid(g_conv) * y_conv
    o_ref[...] = h + _dot(m.astype(BF16), wo_ref[...])


def _mix(u, hp, y_att, conv_w, conv_b, ln_w, ln_b, w_conv_out, norm_w, w_gate, w_out):
    b, l, d = hp.shape
    tm = TM_MIX
    row = pl.BlockSpec((None, tm, d), lambda bi, i: (bi, i, 0))
    halo = pl.BlockSpec((None, HALO, d),
                        lambda bi, i: (bi, jnp.maximum(i * (tm // HALO) - 1, 0), 0))
    vec = _const_spec((1, d))
    return pl.pallas_call(
        _mix_kernel,
        grid=(b, l // tm),
        in_specs=[row, halo, row, row, _const_spec(conv_w.shape), vec, vec, vec,
                  _const_spec(w_conv_out.shape), vec, _const_spec(w_gate.shape),
                  _const_spec(w_out.shape)],
        out_specs=row,
        out_shape=jax.ShapeDtypeStruct((b, l, d), F32),
        scratch_shapes=[pltpu.VMEM((tm + HALO, d), F32), pltpu.VMEM((tm, d), BF16)],
        compiler_params=_params(2),
        name="conv_mix",
    )(u, u, hp, y_att, conv_w, conv_b, ln_w, ln_b, w_conv_out, norm_w, w_gate, w_out)


def _ffn_kernel(h_ref, nw_ref, wup_ref, fw_ref, fb_ref, wdn_ref, fnw_ref, o_ref,
                up_ref, act_ref):
    i = pl.program_id(1)
    tm = h_ref.shape[0]
    ff = D_FF
    lc = 256

    @pl.when(i == 0)
    def _():
        up_ref[0:FFN_HALO, :] = jnp.zeros((FFN_HALO, 2 * ff), F32)

    h = h_ref[...]
    n = _rms_rows(h, nw_ref[...]).astype(BF16)
    up_ref[FFN_HALO:, :] = _dot(n, wup_ref[...])

    def conv(c0):
        acc = jnp.broadcast_to(fb_ref[:, c0:c0 + lc], (tm, lc))
        for t in range(FFN_CONV_K):
            r0 = FFN_HALO - (FFN_CONV_K - 1) + t
            acc = acc + fw_ref[t:t + 1, c0:c0 + lc] * up_ref[r0:r0 + tm, c0:c0 + lc]
        return acc

    for c in range(ff // lc):
        gate = conv(c * lc)
        val = conv(ff + c * lc)
        act_ref[:, c * lc:(c + 1) * lc] = (gate * _sigmoid(gate) * val).astype(BF16)

    up_ref[0:FFN_HALO, :] = up_ref[tm:tm + FFN_HALO, :]
    h3 = h + _dot(act_ref[...], wdn_ref[...])
    o_ref[...] = _rms_rows(h3, fnw_ref[...])


def _ffn(h2, norm_w, w_up, ffn_w, ffn_b, w_down, final_w, seq):
    b, l, d = h2.shape
    tm = TM_FFN
    vec = _const_spec((1, d))
    return pl.pallas_call(
        _ffn_kernel,
        grid=(b, l // tm),
        in_specs=[pl.BlockSpec((None, tm, d), lambda bi, i: (bi, i, 0)), vec,
                  _const_spec(w_up.shape), _const_spec(ffn_w.shape),
                  _const_spec(ffn_b.shape), _const_spec(w_down.shape), vec],
        out_specs=pl.BlockSpec((None, tm, d), lambda bi, i: (bi, jnp.maximum(i - 1, 0), 0)),
        out_shape=jax.ShapeDtypeStruct((b, seq, d), F32),
        scratch_shapes=[pltpu.VMEM((tm + FFN_HALO, 2 * D_FF), F32),
                        pltpu.VMEM((tm, D_FF), BF16)],
        compiler_params=_params(2),
        name="ffn",
    )(h2, norm_w, w_up, ffn_w, ffn_b, w_down, final_w)


def kernel(x, meta_tokens, norm_mix_w, w_in, lambda_q1, lambda_k1, lambda_q2, lambda_k2,
           subln_w, conv_dw_w, conv_dw_b, conv_ln_w, conv_ln_b, w_conv_out, w_out,
           norm_ffn_w, w_up, ffn_dw_w, ffn_dw_b, w_down, norm_final_w):
    b, s, d = x.shape
    assert d == D_MODEL and s % TM_FFN == 0 and FRONT == TM_FFN
    assert w_in.shape[0] == 1, "single-layer trunk"
    l = s + FRONT
    assert l % TM_IN == 0 and l % TM_MIX == 0 and l % TQ == 0 and TQ == TK == FRONT

    meta = jnp.broadcast_to(meta_tokens[None].astype(x.dtype), (b, N_META, d))
    hp = jnp.concatenate([jnp.zeros((b, FRONT - N_META, d), x.dtype), meta, x], axis=1)

    qkvu_cols = 3 * d + 2 * d
    w_qkvu = w_in[0, :, :qkvu_cols].astype(BF16)
    w_gate = w_in[0, :, qkvu_cols:].astype(BF16)
    vec = lambda a: a.reshape(1, -1).astype(F32)

    q, k, v, u = _in_proj(hp, vec(norm_mix_w[0]), w_qkvu)
    y_att = _attention(q, k, v, vec(lambda_q1[0]), vec(lambda_k1[0]), vec(lambda_q2[0]),
                       vec(lambda_k2[0]), vec(subln_w[0]))
    h2 = _mix(u, hp, y_att, conv_dw_w[0].astype(F32), vec(conv_dw_b[0]), vec(conv_ln_w[0]),
              vec(conv_ln_b[0]), w_conv_out[0].astype(BF16), vec(norm_mix_w[0]), w_gate,
              w_out[0].astype(BF16))
    return _ffn(h2, vec(norm_ffn_w[0]), w_up[0].astype(BF16), ffn_dw_w[0].astype(F32),
                vec(ffn_dw_b[0]), w_down[0].astype(BF16), vec(norm_final_w), s)
```

```python
import functools
import math

import jax
import jax.numpy as jnp
from jax import lax
from jax.experimental import pallas as pl
from jax.experimental.pallas import tpu as pltpu

F32 = jnp.float32
BF16 = jnp.bfloat16

D_MODEL = 1024
N_META = 16
CHUNK = 64
N_HEADS = 8
HEAD_DIM = 64
V_HEAD_DIM = 2 * HEAD_DIM
CONV_K = 31
D_FF = 2816
FFN_CONV_K = 3
NEG_INF = -1e30
LAMBDA_INIT = 0.8 - 0.6 * math.exp(0.0)

FRONT = 512
SKIP = 256
HALO = 32
FFN_HALO = 8
CONV_ROWS = 32

TM_IN = 512
TM_ROW = 256
TQ = 256
TK = 512
VMEM_LIMIT = 56 * 1024 * 1024


def _dot(a, b):
    return jnp.dot(a, b, preferred_element_type=F32)


def _sigmoid(x):
    return 1.0 / (1.0 + jnp.exp(-x))


def _rms_rows(x, w, eps=1e-6):
    return x * lax.rsqrt(jnp.mean(x * x, axis=-1, keepdims=True) + eps) * w


def _params(n_axes):
    return pltpu.CompilerParams(
        dimension_semantics=("arbitrary",) * n_axes, vmem_limit_bytes=VMEM_LIMIT)


def _const_spec(shape):
    return pl.BlockSpec(shape, lambda *_: (0,) * len(shape))


def _in_proj_kernel(h_ref, nw_ref, w_ref, q_ref, k_ref, vt_ref, u_ref):
    d = D_MODEL
    n = _rms_rows(h_ref[...], nw_ref[...]).astype(BF16)
    q_ref[...] = (_dot(n, w_ref[:, 0:d]) * (HEAD_DIM ** -0.5)).astype(BF16)
    k_ref[...] = _dot(n, w_ref[:, d:2 * d]).astype(BF16)
    v = _dot(n, w_ref[:, 2 * d:3 * d])
    for t in range(vt_ref.shape[0]):
        vt_ref[t] = v[t * TK:(t + 1) * TK, :].T.astype(BF16)
    a = _dot(n, w_ref[:, 3 * d:4 * d])
    g = _dot(n, w_ref[:, 4 * d:5 * d])
    u_ref[...] = a * _sigmoid(g)


def _in_proj(hp, norm_w, w_qkvu):
    b, l, d = hp.shape
    tm = TM_IN
    row = pl.BlockSpec((None, tm, d), lambda bi, i: (bi, i, 0))
    vt = pl.BlockSpec((None, tm // TK, d, TK), lambda bi, i: (bi, i, 0, 0))
    return pl.pallas_call(
        _in_proj_kernel,
        grid=(b, l // tm),
        in_specs=[row, _const_spec((1, d)), _const_spec(w_qkvu.shape)],
        out_specs=[row, row, vt, row],
        out_shape=[jax.ShapeDtypeStruct((b, l, d), BF16)] * 2
        + [jax.ShapeDtypeStruct((b, l // TK, d, TK), BF16),
           jax.ShapeDtypeStruct((b, l, d), F32)],
        compiler_params=_params(2),
        name="in_proj",
    )(hp, norm_w, w_qkvu)


_PLAIN, _DIAG, _FRONT_KEYS = 0, 1, 2


def _attn_steps(n_qt):
    return sum(i // 2 + 1 for i in range(1, n_qt))


def _attn_kernel(q_ref, k_ref, vt_ref, lq1_ref, lk1_ref, lq2_ref, lk2_ref, sw_ref, o_ref,
                 qst_ref, tbl_ref, s0_ref, s1_ref, mx0_ref, mx1_ref, a0_ref, a1_ref,
                 m_ref, l_ref, acc_ref, *, n_qt):
    tq, tk = TQ, TK
    n_steps = _attn_steps(n_qt)
    slots = ((s0_ref, mx0_ref, a0_ref), (s1_ref, mx1_ref, a1_ref))

    key = lax.broadcasted_iota(jnp.int32, (tk, V_HEAD_DIM), 0)
    lane = lax.broadcasted_iota(jnp.int32, (tk, V_HEAD_DIM), 1)
    chunk_lane = lane < tk // CHUNK

    def bias(masked):
        return jnp.where(chunk_lane & masked, NEG_INF, 0.0).astype(BF16)

    tbl_ref[_PLAIN] = jnp.zeros((tk, V_HEAD_DIM), BF16)
    tbl_ref[_DIAG] = bias(key // CHUNK > lane)
    tbl_ref[_FRONT_KEYS] = bias(key < tk - N_META)

    o_ref[0:tq, :] = jnp.zeros((tq, V_HEAD_DIM), BF16)
    acc_ref[...] = jnp.zeros(acc_ref.shape, F32)
    m_ref[...] = jnp.full(m_ref.shape, NEG_INF, F32)
    l_ref[...] = jnp.zeros(l_ref.shape, F32)

    def build_queries(i):
        qt = q_ref[pl.ds(pl.multiple_of(i * tq, tq), tq), :].astype(F32).T
        dim = lax.broadcasted_iota(jnp.int32, qt.shape, 0)
        qry = lax.broadcasted_iota(jnp.int32, qt.shape, 1)
        chunk = (i & (tk // tq - 1)) * (tq // CHUNK) + qry // CHUNK
        onehot = jnp.where(dim == chunk, 1.0, 0.0).astype(BF16)
        qst_ref[0:V_HEAD_DIM, 0:tq] = jnp.where(dim < HEAD_DIM, qt, 0.0).astype(BF16)
        qst_ref[0:V_HEAD_DIM, tq:] = jnp.where(dim >= HEAD_DIM, qt, 0.0).astype(BF16)
        qst_ref[V_HEAD_DIM:, 0:tq] = onehot
        qst_ref[V_HEAD_DIM:, tq:] = onehot

    def score_stage(i, t, slot):
        s_ref, mx_ref, a_ref = slots[slot]
        kt = k_ref[pl.ds(pl.multiple_of(t * tk, tk), tk), :]
        kind = jnp.where(t == 0, _FRONT_KEYS, jnp.where(t == i >> 1, _DIAG, _PLAIN))
        s = _dot(jnp.concatenate([kt, tbl_ref[kind]], axis=1), qst_ref[...])
        s_ref[...] = s
        m_prev = jnp.where(t == 0, NEG_INF, m_ref[...])
        m_new = jnp.maximum(m_prev, jnp.max(s, axis=0, keepdims=True))
        mx_ref[...] = m_new
        a_ref[...] = jnp.exp(m_prev - m_new)
        m_ref[...] = m_new

    def softmax_value_stage(t, slot):
        s_ref, mx_ref, a_ref = slots[slot]
        alpha = a_ref[...]
        p = jnp.exp(s_ref[...] - mx_ref[...])
        l_ref[...] = alpha * l_ref[...] + jnp.sum(p, axis=0, keepdims=True)
        acc_ref[...] = alpha * acc_ref[...] + _dot(vt_ref[t], p.astype(BF16))

    lam = (jnp.exp(jnp.sum(lq1_ref[...] * lk1_ref[...], axis=-1, keepdims=True))
           - jnp.exp(jnp.sum(lq2_ref[...] * lk2_ref[...], axis=-1, keepdims=True))
           + LAMBDA_INIT)

    def finalize(i):
        l = l_ref[...]
        o = acc_ref[:, 0:tq] / l[:, 0:tq] - lam * (acc_ref[:, tq:] / l[:, tq:])
        o = o * lax.rsqrt(jnp.mean(o * o, axis=0, keepdims=True) + 1e-6)
        y = o.T * sw_ref[...] * (1.0 - LAMBDA_INIT)
        o_ref[pl.ds(pl.multiple_of(i * tq, tq), tq), :] = y.astype(BF16)

    def advance(i, t):
        wrap = t == i >> 1
        return jnp.where(wrap, i + 1, i), jnp.where(wrap, 0, t + 1)

    def iteration(carry, cur):
        ia, ta, ib, tb = carry

        @pl.when(jnp.logical_and(ta == 0, ia < n_qt))
        def _():
            build_queries(ia)

        softmax_value_stage(tb, cur)
        score_stage(ia, ta, 1 - cur)

        @pl.when(tb == ib >> 1)
        def _():
            finalize(ib)

        return advance(ia, ta) + (ia, ta)

    zero, one = jnp.int32(0), jnp.int32(1)
    build_queries(one)
    score_stage(one, zero, 0)
    carry = advance(one, zero) + (one, zero)
    carry = lax.fori_loop(0, n_steps // 2, lambda _, c: iteration(iteration(c, 0), 1), carry)
    for n in range(n_steps - n_steps % 2, n_steps):
        carry = iteration(carry, n % 2)


def _attention(q, k, vt, lq1, lk1, lq2, lk2, subln_w):
    b, l, d = q.shape
    n_qt = l // TQ
    col = pl.BlockSpec((None, l, V_HEAD_DIM), lambda bi, h: (bi, 0, h))
    vtspec = pl.BlockSpec((None, l // TK, V_HEAD_DIM, TK), lambda bi, h: (bi, 0, h, 0))
    lspec = _const_spec((1, HEAD_DIM))
    nq = 2 * TQ
    per_slot = lambda shape, dtype: [pltpu.VMEM(shape, dtype)] * 2
    return pl.pallas_call(
        functools.partial(_attn_kernel, n_qt=n_qt),
        grid=(b, N_HEADS),
        in_specs=[col, col, vtspec, lspec, lspec, lspec, lspec, _const_spec((1, V_HEAD_DIM))],
        out_specs=col,
        out_shape=jax.ShapeDtypeStruct((b, l, d), BF16),
        scratch_shapes=[
            pltpu.VMEM((2 * V_HEAD_DIM, nq), BF16),
            pltpu.VMEM((3, TK, V_HEAD_DIM), BF16),
            *per_slot((TK, nq), F32),
            *per_slot((1, nq), F32),
            *per_slot((1, nq), F32),
            pltpu.VMEM((1, nq), F32),
            pltpu.VMEM((1, nq), F32),
            pltpu.VMEM((V_HEAD_DIM, nq), F32),
        ],
        compiler_params=_params(2),
        name="diff_attn",
    )(q, k, vt, lq1, lk1, lq2, lk2, subln_w)


def _mix_kernel(u_ref, uh_ref, h_ref, ya_ref, cw_ref, cb_ref, lnw_ref, lnb_ref, wco_ref,
                nw_ref, wg_ref, wo_ref, o_ref, ubuf_ref, act_ref):
    tm, d = act_ref.shape

    ubuf_ref[0:HALO, :] = uh_ref[...]
    ubuf_ref[HALO:, :] = u_ref[...]

    def conv_block(rb, carry):
        base = pl.multiple_of(rb * CONV_ROWS, CONV_ROWS)
        win = ubuf_ref[pl.ds(base, CONV_ROWS + HALO), :]
        acc = jnp.broadcast_to(cb_ref[...], (CONV_ROWS, d))
        for t in range(CONV_K):
            r0 = HALO - CONV_K + 1 + t
            acc = acc + cw_ref[t:t + 1, :] * win[r0:r0 + CONV_ROWS, :]
        mu = jnp.mean(acc, axis=-1, keepdims=True)
        xc = acc - mu
        y = xc * lax.rsqrt(jnp.mean(xc * xc, axis=-1, keepdims=True) + 1e-5)
        y = y * lnw_ref[...] + lnb_ref[...]
        act_ref[pl.ds(base, CONV_ROWS), :] = (y * _sigmoid(y)).astype(BF16)
        return carry

    lax.fori_loop(0, tm // CONV_ROWS, conv_block, 0)

    y_conv = _dot(act_ref[...], wco_ref[...])
    h = h_ref[...]
    n = _rms_rows(h, nw_ref[...]).astype(BF16)
    g_att = _dot(n, wg_ref[:, 0:d])
    g_conv = _dot(n, wg_ref[:, d:2 * d])
    m = _sigmoid(g_att) * ya_ref[...].astype(F32) + _sigmoid(g_conv) * y_conv
    o_ref[...] = h + _dot(m.astype(BF16), wo_ref[...])


def _mix(u, hp, y_att, conv_w, conv_b, ln_w, ln_b, w_conv_out, norm_w, w_gate, w_out):
    b, l, d = hp.shape
    tm = TM_ROW
    skip = SKIP // tm
    row = pl.BlockSpec((None, tm, d), lambda bi, i: (bi, i + skip, 0))
    halo = pl.BlockSpec((None, HALO, d), lambda bi, i: (bi, (i + skip) * (tm // HALO) - 1, 0))
    vec = _const_spec((1, d))
    return pl.pallas_call(
        _mix_kernel,
        grid=(b, l // tm - skip),
        in_specs=[row, halo, row, row, _const_spec(conv_w.shape), vec, vec, vec,
                  _const_spec(w_conv_out.shape), vec, _const_spec(w_gate.shape),
                  _const_spec(w_out.shape)],
        out_specs=row,
        out_shape=jax.ShapeDtypeStruct((b, l, d), F32),
        scratch_shapes=[pltpu.VMEM((tm + HALO, d), F32), pltpu.VMEM((tm, d), BF16)],
        compiler_params=_params(2),
        name="conv_mix",
    )(u, u, hp, y_att, conv_w, conv_b, ln_w, ln_b, w_conv_out, norm_w, w_gate, w_out)


def _ffn_kernel(h_ref, nw_ref, wup_ref, fw_ref, fb_ref, wdn_ref, fnw_ref, o_ref,
                up_ref, act_ref):
    i = pl.program_id(1)
    tm = h_ref.shape[0]
    ff = D_FF
    lc = 256

    @pl.when(i == 0)
    def _():
        up_ref[0:FFN_HALO, :] = jnp.zeros((FFN_HALO, 2 * ff), F32)

    h = h_ref[...]
    n = _rms_rows(h, nw_ref[...]).astype(BF16)
    up_ref[FFN_HALO:, :] = _dot(n, wup_ref[...])

    def conv(c0):
        acc = jnp.broadcast_to(fb_ref[:, c0:c0 + lc], (tm, lc))
        for t in range(FFN_CONV_K):
            r0 = FFN_HALO - (FFN_CONV_K - 1) + t
            acc = acc + fw_ref[t:t + 1, c0:c0 + lc] * up_ref[r0:r0 + tm, c0:c0 + lc]
        return acc

    for c in range(ff // lc):
        gate = conv(c * lc)
        val = conv(ff + c * lc)
        act_ref[:, c * lc:(c + 1) * lc] = (gate * _sigmoid(gate) * val).astype(BF16)

    up_ref[0:FFN_HALO, :] = up_ref[tm:tm + FFN_HALO, :]
    h3 = h + _dot(act_ref[...], wdn_ref[...])
    o_ref[...] = _rms_rows(h3, fnw_ref[...])


def _ffn(h2, norm_w, w_up, ffn_w, ffn_b, w_down, final_w, seq):
    b, l, d = h2.shape
    tm = TM_ROW
    skip = SKIP // tm
    front_tiles = (FRONT - SKIP) // tm
    vec = _const_spec((1, d))
    return pl.pallas_call(
        _ffn_kernel,
        grid=(b, l // tm - skip),
        in_specs=[pl.BlockSpec((None, tm, d), lambda bi, i: (bi, i + skip, 0)), vec,
                  _const_spec(w_up.shape), _const_spec(ffn_w.shape),
                  _const_spec(ffn_b.shape), _const_spec(w_down.shape), vec],
        out_specs=pl.BlockSpec((None, tm, d),
                               lambda bi, i: (bi, jnp.maximum(i - front_tiles, 0), 0)),
        out_shape=jax.ShapeDtypeStruct((b, seq, d), F32),
        scratch_shapes=[pltpu.VMEM((tm + FFN_HALO, 2 * D_FF), F32),
                        pltpu.VMEM((tm, D_FF), BF16)],
        compiler_params=_params(2),
        name="ffn",
    )(h2, norm_w, w_up, ffn_w, ffn_b, w_down, final_w)


def kernel(x, meta_tokens, norm_mix_w, w_in, lambda_q1, lambda_k1, lambda_q2, lambda_k2,
           subln_w, conv_dw_w, conv_dw_b, conv_ln_w, conv_ln_b, w_conv_out, w_out,
           norm_ffn_w, w_up, ffn_dw_w, ffn_dw_b, w_down, norm_final_w):
    b, s, d = x.shape
    assert d == D_MODEL and w_in.shape[0] == 1, "single-layer trunk"
    assert s % TK == 0 and FRONT == TK == 2 * TQ and SKIP == TM_ROW and FRONT - SKIP == TM_ROW
    assert SKIP + HALO <= FRONT - N_META and TK // CHUNK <= V_HEAD_DIM
    l = s + FRONT
    assert l % TM_IN == 0

    meta = jnp.broadcast_to(meta_tokens[None].astype(x.dtype), (b, N_META, d))
    hp = jnp.concatenate([jnp.zeros((b, FRONT - N_META, d), x.dtype), meta, x], axis=1)

    qkvu_cols = 3 * d + 2 * d
    w_qkvu = w_in[0, :, :qkvu_cols].astype(BF16)
    w_gate = w_in[0, :, qkvu_cols:].astype(BF16)
    vec = lambda a: a.reshape(1, -1).astype(F32)

    q, k, vt, u = _in_proj(hp, vec(norm_mix_w[0]), w_qkvu)
    y_att = _attention(q, k, vt, vec(lambda_q1[0]), vec(lambda_k1[0]), vec(lambda_q2[0]),
                       vec(lambda_k2[0]), vec(subln_w[0]))
    h2 = _mix(u, hp, y_att, conv_dw_w[0].astype(F32), vec(conv_dw_b[0]), vec(conv_ln_w[0]),
              vec(conv_ln_b[0]), w_conv_out[0].astype(BF16), vec(norm_mix_w[0]), w_gate,
              w_out[0].astype(BF16))
    return _ffn(h2, vec(norm_ffn_w[0]), w_up[0].astype(BF16), ffn_dw_w[0].astype(F32),
                vec(ffn_dw_b[0]), w_down[0].astype(BF16), vec(norm_final_w), s)
```

```python
import functools
import math

import jax
import jax.numpy as jnp
from jax import lax
from jax.experimental import pallas as pl
from jax.experimental.pallas import tpu as pltpu

F32 = jnp.float32
BF16 = jnp.bfloat16

D_MODEL = 1024
N_META = 16
CHUNK = 64
N_HEADS = 8
HEAD_DIM = 64
V_HEAD_DIM = 2 * HEAD_DIM
CONV_K = 31
D_FF = 2816
FFN_CONV_K = 3
NEG_INF = -1e30
LAMBDA_INIT = 0.8 - 0.6 * math.exp(0.0)
LOG2_E = math.log2(math.e)

SUBLANES = 8

FRONT = 512
SKIP = 256
HALO = 32
FFN_HALO = 8
CONV_ROWS = 32
DENOM_ROWS = 16
ATTN_W = 4
ATTN_TILE_BUFS = 4

TM_IN = 512
TM_ROW = 256
TQ = 256
TK = 512
VMEM_LIMIT = 56 * 1024 * 1024


def _dot(a, b):
    return jnp.dot(a, b, preferred_element_type=F32)


def _sigmoid(x):
    return 1.0 / (1.0 + jnp.exp(-x))


def _rms_rows(x, w, eps=1e-6):
    return x * lax.rsqrt(jnp.mean(x * x, axis=-1, keepdims=True) + eps) * w


def _params(n_axes):
    return pltpu.CompilerParams(
        dimension_semantics=("arbitrary",) * n_axes, vmem_limit_bytes=VMEM_LIMIT)


def _const_spec(shape):
    return pl.BlockSpec(shape, lambda *_: (0,) * len(shape))


def _in_proj_kernel(h_ref, nw_ref, w_ref, q_ref, k_ref, vt_ref, u_ref):
    d = D_MODEL
    n = _rms_rows(h_ref[...], nw_ref[...]).astype(BF16)
    q_ref[...] = (_dot(n, w_ref[:, 0:d]) * (HEAD_DIM ** -0.5 * LOG2_E)).astype(BF16)
    k_ref[...] = _dot(n, w_ref[:, d:2 * d]).astype(BF16)
    v = _dot(n, w_ref[:, 2 * d:3 * d])
    for t in range(vt_ref.shape[0]):
        vt_ref[t] = v[t * TK:(t + 1) * TK, :].T.astype(BF16)
    a = _dot(n, w_ref[:, 3 * d:4 * d])
    g = _dot(n, w_ref[:, 4 * d:5 * d])
    u_ref[...] = a * _sigmoid(g)


def _in_proj(hp, norm_w, w_qkvu):
    b, l, d = hp.shape
    tm = TM_IN
    row = pl.BlockSpec((None, tm, d), lambda bi, i: (bi, i, 0))
    vt = pl.BlockSpec((None, tm // TK, d, TK), lambda bi, i: (bi, i, 0, 0))
    return pl.pallas_call(
        _in_proj_kernel,
        grid=(b, l // tm),
        in_specs=[row, _const_spec((1, d)), _const_spec(w_qkvu.shape)],
        out_specs=[row, row, vt, row],
        out_shape=[jax.ShapeDtypeStruct((b, l, d), BF16)] * 2
        + [jax.ShapeDtypeStruct((b, l // TK, d, TK), BF16),
           jax.ShapeDtypeStruct((b, l, d), F32)],
        compiler_params=_params(2),
        name="in_proj",
    )(hp, norm_w, w_qkvu)


_PLAIN, _DIAG, _FRONT_KEYS = 0, 1, 2


def _attn_steps(n_qt):
    return sum(i // 2 + 1 for i in range(1, n_qt))


def _attn_kernel(q_ref, k_ref, vt_ref, lq1_ref, lk1_ref, lq2_ref, lk2_ref, sw_ref, o_ref,
                 qst_ref, tbl_ref, m_ref, acc_ref, *slot_refs, n_qt):
    tq, tk, w = TQ, TK, ATTN_W
    n_steps = _attn_steps(n_qt)
    n_slots = 2 * w
    slots = tuple(zip(slot_refs[0:n_slots], slot_refs[n_slots:2 * n_slots], slot_refs[2 * n_slots:]))

    key = lax.broadcasted_iota(jnp.int32, (tk, V_HEAD_DIM), 0)
    lane = lax.broadcasted_iota(jnp.int32, (tk, V_HEAD_DIM), 1)
    chunk_lane = lane < tk // CHUNK

    def bias(masked):
        return jnp.where(chunk_lane & masked, NEG_INF, 0.0).astype(BF16)

    tbl_ref[_PLAIN] = jnp.zeros((tk, V_HEAD_DIM), BF16)
    tbl_ref[_DIAG] = bias(key // CHUNK > lane)
    tbl_ref[_FRONT_KEYS] = bias(key < tk - N_META)

    o_ref[0:tq, :] = jnp.zeros((tq, V_HEAD_DIM), BF16)
    acc_ref[...] = jnp.zeros(acc_ref.shape, F32)
    m_ref[...] = jnp.full(m_ref.shape, NEG_INF, F32)
    ones_rows = jnp.ones((DENOM_ROWS, tk), BF16)

    def build_queries(i):
        qt = q_ref[pl.ds(pl.multiple_of(i * tq, tq), tq), :].astype(F32).T
        dim = lax.broadcasted_iota(jnp.int32, qt.shape, 0)
        qry = lax.broadcasted_iota(jnp.int32, qt.shape, 1)
        chunk = (i & (tk // tq - 1)) * (tq // CHUNK) + qry // CHUNK
        onehot = jnp.where(dim == chunk, 1.0, 0.0).astype(BF16)
        par = i & (ATTN_TILE_BUFS - 1)
        qst_ref[par, 0:V_HEAD_DIM, 0:tq] = jnp.where(dim < HEAD_DIM, qt, 0.0).astype(BF16)
        qst_ref[par, 0:V_HEAD_DIM, tq:] = jnp.where(dim >= HEAD_DIM, qt, 0.0).astype(BF16)
        qst_ref[par, V_HEAD_DIM:, 0:tq] = onehot
        qst_ref[par, V_HEAD_DIM:, tq:] = onehot

    def score_stage(i, t, slot):
        s_ref, mx_ref, a_ref = slots[slot]
        kt = k_ref[pl.ds(pl.multiple_of(t * tk, tk), tk), :]
        kind = jnp.where(t == 0, _FRONT_KEYS, jnp.where(t == i >> 1, _DIAG, _PLAIN))
        s = _dot(jnp.concatenate([kt, tbl_ref[kind]], axis=1),
                 qst_ref[i & (ATTN_TILE_BUFS - 1)])
        s_ref[...] = s
        m_prev = jnp.where(t == 0, NEG_INF, m_ref[...])
        m_new = jnp.maximum(m_prev, jnp.max(s, axis=0, keepdims=True))
        mx_ref[...] = m_new
        a_ref[...] = jnp.exp2(m_prev - m_new)
        m_ref[...] = m_new

    def softmax_value_stage(i, t, slot):
        s_ref, mx_ref, a_ref = slots[slot]
        p = jnp.exp2(s_ref[...] - mx_ref[...]).astype(BF16)
        v_aug = jnp.concatenate([vt_ref[t], ones_rows], axis=0)
        par = i & (ATTN_TILE_BUFS - 1)
        acc_ref[par] = a_ref[...] * acc_ref[par] + _dot(v_aug, p)

    lam = (jnp.exp(jnp.sum(lq1_ref[...] * lk1_ref[...], axis=-1, keepdims=True))
           - jnp.exp(jnp.sum(lq2_ref[...] * lk2_ref[...], axis=-1, keepdims=True))
           + LAMBDA_INIT)

    def finalize(i):
        par = i & (ATTN_TILE_BUFS - 1)
        l = acc_ref[par, V_HEAD_DIM:V_HEAD_DIM + 1, :]
        acc = acc_ref[par, 0:V_HEAD_DIM, :]
        o = acc[:, 0:tq] / l[:, 0:tq] - lam * (acc[:, tq:] / l[:, tq:])
        o = o * lax.rsqrt(jnp.mean(o * o, axis=0, keepdims=True) + 1e-6)
        y = o.T * sw_ref[...] * (1.0 - LAMBDA_INIT)
        o_ref[pl.ds(pl.multiple_of(i * tq, tq), tq), :] = y.astype(BF16)

    def advance(i, t):
        wrap = t == i >> 1
        return jnp.where(wrap, i + 1, i), jnp.where(wrap, 0, t + 1)

    def build_if_first(i, t):
        @pl.when(jnp.logical_and(t == 0, i < n_qt))
        def _():
            build_queries(i)

    def finalize_if_last(i, t):
        @pl.when(t == i >> 1)
        def _():
            finalize(i)

    def following(step, count):
        steps = [step]
        for _ in range(count - 1):
            steps.append(advance(*steps[-1]))
        return steps[:count]

    def block(first, base):
        steps = following(first, 2 * w)
        for step in steps[w:]:
            build_if_first(*step)
        for j in range(w):
            score_stage(*steps[w + j], (base + w + j) % n_slots)
            softmax_value_stage(*steps[j], base + j)
        for step in steps[:w]:
            finalize_if_last(*step)
        return steps[w]

    start = (jnp.int32(1), jnp.int32(0))
    for j, step in enumerate(following(start, w)):
        build_if_first(*step)
        score_stage(*step, j)
    n_trips = n_steps // n_slots
    first = lax.fori_loop(0, n_trips, lambda _, c: block(block(c, 0), w), start)
    for j, step in enumerate(following(first, n_steps - n_trips * n_slots)):
        if j >= w:
            build_if_first(*step)
            score_stage(*step, j)
        softmax_value_stage(*step, j)
        finalize_if_last(*step)


def _attention(q, k, vt, lq1, lk1, lq2, lk2, subln_w):
    b, l, d = q.shape
    n_qt = l // TQ
    col = pl.BlockSpec((None, l, V_HEAD_DIM), lambda bi, h: (bi, 0, h))
    vtspec = pl.BlockSpec((None, l // TK, V_HEAD_DIM, TK), lambda bi, h: (bi, 0, h, 0))
    lspec = _const_spec((1, HEAD_DIM))
    nq = 2 * TQ
    per_slot = lambda shape, dtype: [pltpu.VMEM(shape, dtype)] * (2 * ATTN_W)
    return pl.pallas_call(
        functools.partial(_attn_kernel, n_qt=n_qt),
        grid=(b, N_HEADS),
        in_specs=[col, col, vtspec, lspec, lspec, lspec, lspec, _const_spec((1, V_HEAD_DIM))],
        out_specs=col,
        out_shape=jax.ShapeDtypeStruct((b, l, d), BF16),
        scratch_shapes=[
            pltpu.VMEM((ATTN_TILE_BUFS, 2 * V_HEAD_DIM, nq), BF16),
            pltpu.VMEM((3, TK, V_HEAD_DIM), BF16),
            pltpu.VMEM((1, nq), F32),
            pltpu.VMEM((ATTN_TILE_BUFS, V_HEAD_DIM + DENOM_ROWS, nq), F32),
            *per_slot((TK, nq), F32),
            *per_slot((1, nq), F32),
            *per_slot((1, nq), F32),
        ],
        compiler_params=_params(2),
        name="diff_attn",
    )(q, k, vt, lq1, lk1, lq2, lk2, subln_w)


def _mix_kernel(u_ref, uh_ref, h_ref, ya_ref, cw_ref, cb_ref, lnw_ref, lnb_ref, wco_ref,
                nw_ref, wg_ref, wo_ref, o_ref, ubuf_ref, act_ref):
    tm, d = act_ref.shape

    ubuf_ref[0:HALO, :] = uh_ref[...]
    ubuf_ref[HALO:, :] = u_ref[...]

    h = h_ref[...]
    n = _rms_rows(h, nw_ref[...]).astype(BF16)
    gate_att = _sigmoid(_dot(n, wg_ref[:, 0:d])) * ya_ref[...].astype(F32)
    gate_conv = _sigmoid(_dot(n, wg_ref[:, d:2 * d]))

    for base in range(0, tm, CONV_ROWS):
        win = ubuf_ref[base:base + CONV_ROWS + HALO, :]
        groups = (CONV_ROWS // SUBLANES, SUBLANES, d)
        acc = jnp.broadcast_to(cb_ref[...][None], groups)
        for c in range(SUBLANES):
            taps = [t for t in range(CONV_K) if (HALO - CONV_K + 1 + t) % SUBLANES == c]
            span = max(HALO - CONV_K + 1 + t for t in taps) - c + CONV_ROWS
            shifted = win[c:c + span, :]
            for t in taps:
                q = HALO - CONV_K + 1 + t - c
                acc = acc + cw_ref[t][None] * shifted[q:q + CONV_ROWS, :].reshape(groups)
        acc = acc.reshape(CONV_ROWS, d)
        mu = jnp.mean(acc, axis=-1, keepdims=True)
        xc = acc - mu
        y = xc * lax.rsqrt(jnp.mean(xc * xc, axis=-1, keepdims=True) + 1e-5)
        y = y * lnw_ref[...] + lnb_ref[...]
        act_ref[base:base + CONV_ROWS, :] = (y * _sigmoid(y)).astype(BF16)

    m = gate_att + gate_conv * _dot(act_ref[...], wco_ref[...])
    o_ref[...] = h + _dot(m.astype(BF16), wo_ref[...])


def _mix(u, hp, y_att, conv_w, conv_b, ln_w, ln_b, w_conv_out, norm_w, w_gate, w_out):
    b, l, d = hp.shape
    tm = TM_ROW
    skip = SKIP // tm
    row = pl.BlockSpec((None, tm, d), lambda bi, i: (bi, i + skip, 0))
    halo = pl.BlockSpec((None, HALO, d), lambda bi, i: (bi, (i + skip) * (tm // HALO) - 1, 0))
    vec = _const_spec((1, d))
    return pl.pallas_call(
        _mix_kernel,
        grid=(b, l // tm - skip),
        in_specs=[row, halo, row, row, _const_spec(conv_w.shape), _const_spec(conv_b.shape), vec, vec,
                  _const_spec(w_conv_out.shape), vec, _const_spec(w_gate.shape),
                  _const_spec(w_out.shape)],
        out_specs=row,
        out_shape=jax.ShapeDtypeStruct((b, l, d), F32),
        scratch_shapes=[pltpu.VMEM((tm + HALO, d), F32), pltpu.VMEM((tm, d), BF16)],
        compiler_params=_params(2),
        name="conv_mix",
    )(u, u, hp, y_att, conv_w, conv_b, ln_w, ln_b, w_conv_out, norm_w, w_gate, w_out)


def _ffn_kernel(h_ref, nw_ref, wup_ref, fw_ref, fb_ref, wdn_ref, fnw_ref, o_ref,
                up_ref, act_ref):
    i = pl.program_id(1)
    tm = h_ref.shape[0]
    ff = D_FF
    lc = 256

    @pl.when(i == 0)
    def _():
        up_ref[0:FFN_HALO, :] = jnp.zeros((FFN_HALO, 2 * ff), F32)

    h = h_ref[...]
    n = _rms_rows(h, nw_ref[...]).astype(BF16)
    up_ref[FFN_HALO:, :] = _dot(n, wup_ref[...])

    def conv(c0):
        acc = jnp.broadcast_to(fb_ref[:, c0:c0 + lc], (tm, lc))
        for t in range(FFN_CONV_K):
            r0 = FFN_HALO - (FFN_CONV_K - 1) + t
            acc = acc + fw_ref[t:t + 1, c0:c0 + lc] * up_ref[r0:r0 + tm, c0:c0 + lc]
        return acc

    for c in range(ff // lc):
        gate = conv(c * lc)
        val = conv(ff + c * lc)
        act_ref[:, c * lc:(c + 1) * lc] = (gate * _sigmoid(gate) * val).astype(BF16)

    up_ref[0:FFN_HALO, :] = up_ref[tm:tm + FFN_HALO, :]
    h3 = h + _dot(act_ref[...], wdn_ref[...])
    o_ref[...] = _rms_rows(h3, fnw_ref[...])


def _ffn(h2, norm_w, w_up, ffn_w, ffn_b, w_down, final_w, seq):
    b, l, d = h2.shape
    tm = TM_ROW
    skip = SKIP // tm
    front_tiles = (FRONT - SKIP) // tm
    vec = _const_spec((1, d))
    return pl.pallas_call(
        _ffn_kernel,
        grid=(b, l // tm - skip),
        in_specs=[pl.BlockSpec((None, tm, d), lambda bi, i: (bi, i + skip, 0)), vec,
                  _const_spec(w_up.shape), _const_spec(ffn_w.shape),
                  _const_spec(ffn_b.shape), _const_spec(w_down.shape), vec],
        out_specs=pl.BlockSpec((None, tm, d),
                               lambda bi, i: (bi, jnp.maximum(i - front_tiles, 0), 0)),
        out_shape=jax.ShapeDtypeStruct((b, seq, d), F32),
        scratch_shapes=[pltpu.VMEM((tm + FFN_HALO, 2 * D_FF), F32),
                        pltpu.VMEM((tm, D_FF), BF16)],
        compiler_params=_params(2),
        name="ffn",
    )(h2, norm_w, w_up, ffn_w, ffn_b, w_down, final_w)


def kernel(x, meta_tokens, norm_mix_w, w_in, lambda_q1, lambda_k1, lambda_q2, lambda_k2,
           subln_w, conv_dw_w, conv_dw_b, conv_ln_w, conv_ln_b, w_conv_out, w_out,
           norm_ffn_w, w_up, ffn_dw_w, ffn_dw_b, w_down, norm_final_w):
    b, s, d = x.shape
    assert d == D_MODEL and w_in.shape[0] == 1, "single-layer trunk"
    assert s % TK == 0 and FRONT == TK == 2 * TQ and SKIP == TM_ROW and FRONT - SKIP == TM_ROW
    assert SKIP + HALO <= FRONT - N_META and TK // CHUNK <= V_HEAD_DIM
    l = s + FRONT
    assert l % TM_IN == 0

    meta = jnp.broadcast_to(meta_tokens[None].astype(x.dtype), (b, N_META, d))
    hp = jnp.concatenate([jnp.zeros((b, FRONT - N_META, d), x.dtype), meta, x], axis=1)

    qkvu_cols = 3 * d + 2 * d
    w_qkvu = w_in[0, :, :qkvu_cols].astype(BF16)
    w_gate = w_in[0, :, qkvu_cols:].astype(BF16)
    vec = lambda a: a.reshape(1, -1).astype(F32)

    q, k, vt, u = _in_proj(hp, vec(norm_mix_w[0]), w_qkvu)
    y_att = _attention(q, k, vt, vec(lambda_q1[0]), vec(lambda_k1[0]), vec(lambda_q2[0]),
                       vec(lambda_k2[0]), vec(subln_w[0]))
    tile_rows = lambda a: jnp.broadcast_to(a.astype(F32)[..., None, :], a.shape[:-1] + (SUBLANES, d))
    h2 = _mix(u, hp, y_att, tile_rows(conv_dw_w[0]), tile_rows(conv_dw_b[0]), vec(conv_ln_w[0]),
              vec(conv_ln_b[0]), w_conv_out[0].astype(BF16), vec(norm_mix_w[0]), w_gate,
              w_out[0].astype(BF16))
    return _ffn(h2, vec(norm_ffn_w[0]), w_up[0].astype(BF16), ffn_dw_w[0].astype(F32),
                vec(ffn_dw_b[0]), w_down[0].astype(BF16), vec(norm_final_w), s)
```

```python
import functools
import math

import jax
import jax.numpy as jnp
from jax import lax
from jax.experimental import pallas as pl
from jax.experimental.pallas import tpu as pltpu

F32 = jnp.float32
BF16 = jnp.bfloat16

D_MODEL = 1024
N_META = 16
CHUNK = 64
N_HEADS = 8
HEAD_DIM = 64
V_HEAD_DIM = 2 * HEAD_DIM
CONV_K = 31
D_FF = 2816
FFN_CONV_K = 3
NEG_INF = -1e30
LAMBDA_INIT = 0.8 - 0.6 * math.exp(0.0)
LOG2_E = math.log2(math.e)

SUBLANES = 8

FRONT = 512
SKIP = 256
HALO = 32
FFN_HALO = 8
CONV_ROWS = 32
DENOM_ROWS = 16
ATTN_W = 8
ATTN_TILE_BUFS = 8

TM_IN = 512
TM_ROW = 256
TQ = 256
TK = 512
VMEM_LIMIT = 56 * 1024 * 1024


def _dot(a, b):
    return jnp.dot(a, b, preferred_element_type=F32)


def _sigmoid(x):
    return 1.0 / (1.0 + jnp.exp(-x))


def _rms_rows(x, w, eps=1e-6):
    return x * lax.rsqrt(jnp.mean(x * x, axis=-1, keepdims=True) + eps) * w


def _params(n_axes):
    return pltpu.CompilerParams(
        dimension_semantics=("arbitrary",) * n_axes, vmem_limit_bytes=VMEM_LIMIT)


def _const_spec(shape):
    return pl.BlockSpec(shape, lambda *_: (0,) * len(shape))


def _front_or_frames(x_ref, meta_ref, is_front):
    tm, d = x_ref.shape
    front = jnp.concatenate([jnp.zeros((tm - N_META, d), F32), meta_ref[...]], axis=0)
    return jnp.where(is_front, front, x_ref[...])


def _in_proj_kernel(x_ref, meta_ref, nw_ref, w_ref, qt_ref, k_ref, vt_ref, u_ref):
    d = D_MODEL
    h = _front_or_frames(x_ref, meta_ref, pl.program_id(1) == 0)
    n = _rms_rows(h, nw_ref[...]).astype(BF16)
    q = _dot(n, w_ref[:, 0:d]) * (HEAD_DIM ** -0.5 * LOG2_E)
    for t in range(qt_ref.shape[0]):
        qt_ref[t] = q[t * TQ:(t + 1) * TQ, :].T.astype(BF16)
    k_ref[...] = _dot(n, w_ref[:, d:2 * d]).astype(BF16)
    v = _dot(n, w_ref[:, 2 * d:3 * d])
    for t in range(vt_ref.shape[0]):
        vt_ref[t] = v[t * TK:(t + 1) * TK, :].T.astype(BF16)
    a = _dot(n, w_ref[:, 3 * d:4 * d])
    g = _dot(n, w_ref[:, 4 * d:5 * d])
    u_ref[...] = a * _sigmoid(g)


def _in_proj(x, meta, norm_w, w_qkvu):
    b, s, d = x.shape
    l = s + FRONT
    tm = TM_IN
    row = pl.BlockSpec((None, tm, d), lambda bi, i: (bi, i, 0))
    frames = pl.BlockSpec((None, tm, d), lambda bi, i: (bi, jnp.maximum(i - 1, 0), 0))
    qt = pl.BlockSpec((None, tm // TQ, d, TQ), lambda bi, i: (bi, i, 0, 0))
    vt = pl.BlockSpec((None, tm // TK, d, TK), lambda bi, i: (bi, i, 0, 0))
    return pl.pallas_call(
        _in_proj_kernel,
        grid=(b, l // tm),
        in_specs=[frames, _const_spec(meta.shape), _const_spec((1, d)), _const_spec(w_qkvu.shape)],
        out_specs=[qt, row, vt, row],
        out_shape=[jax.ShapeDtypeStruct((b, l // TQ, d, TQ), BF16),
                   jax.ShapeDtypeStruct((b, l, d), BF16),
                   jax.ShapeDtypeStruct((b, l // TK, d, TK), BF16),
                   jax.ShapeDtypeStruct((b, l, d), F32)],
        compiler_params=_params(2),
        name="in_proj",
    )(x, meta, norm_w, w_qkvu)


_PLAIN, _DIAG, _FRONT_KEYS, _DIAG_AND_FRONT = 0, 1, 2, 3


def _attn_steps(n_qt):
    return sum(i // 2 + i % 2 for i in range(1, n_qt))


def _attn_kernel(qt_ref, k_ref, vt_ref, lq1_ref, lk1_ref, lq2_ref, lk2_ref, sw_ref, o_ref,
                 qst_ref, tbl_ref, onehot_ref, m_ref, acc_ref, *slot_refs, n_qt):
    tq, tk, w = TQ, TK, ATTN_W
    n_steps = _attn_steps(n_qt)
    n_slots = 2 * w
    slots = tuple(zip(slot_refs[0:n_slots], slot_refs[n_slots:2 * n_slots], slot_refs[2 * n_slots:]))

    key = lax.broadcasted_iota(jnp.int32, (tk, V_HEAD_DIM), 0)
    lane = lax.broadcasted_iota(jnp.int32, (tk, V_HEAD_DIM), 1)
    chunk_lane = lane < tk // CHUNK

    def bias(masked):
        return jnp.where(chunk_lane & masked, NEG_INF, 0.0).astype(BF16)

    tbl_ref[_PLAIN] = jnp.zeros((tk, V_HEAD_DIM), BF16)
    tbl_ref[_DIAG] = bias(key // CHUNK > lane)
    tbl_ref[_FRONT_KEYS] = bias(key < tk - N_META)
    tbl_ref[_DIAG_AND_FRONT] = bias(
        ((key < tk // 2) & (key // CHUNK > lane)) | ((key >= tk // 2) & (key < tk - N_META)))

    dim = lax.broadcasted_iota(jnp.int32, (V_HEAD_DIM, tq), 0)
    qry = lax.broadcasted_iota(jnp.int32, (V_HEAD_DIM, tq), 1)
    for part in range(tk // tq):
        chunk = part * (tq // CHUNK) + qry // CHUNK
        onehot_ref[part] = jnp.where(dim == chunk, 1.0, 0.0).astype(BF16)

    o_ref[0:tq, :] = jnp.zeros((tq, V_HEAD_DIM), BF16)
    acc_ref[...] = jnp.zeros(acc_ref.shape, F32)
    m_ref[...] = jnp.full(m_ref.shape, NEG_INF, F32)
    ones_rows = jnp.ones((DENOM_ROWS, tk), BF16)

    def build_queries(i):
        qt = qt_ref[i]
        zeros = jnp.zeros((HEAD_DIM, tq), BF16)
        onehot = onehot_ref[i & (tk // tq - 1)]
        par = i & (ATTN_TILE_BUFS - 1)
        qst_ref[par, 0:HEAD_DIM, 0:tq] = qt[0:HEAD_DIM]
        qst_ref[par, HEAD_DIM:V_HEAD_DIM, 0:tq] = zeros
        qst_ref[par, 0:HEAD_DIM, tq:] = zeros
        qst_ref[par, HEAD_DIM:V_HEAD_DIM, tq:] = qt[HEAD_DIM:]
        qst_ref[par, V_HEAD_DIM:, 0:tq] = onehot
        qst_ref[par, V_HEAD_DIM:, tq:] = onehot

    half = tk // 2

    def merged(i, t):
        return jnp.logical_and(i & 1 == 0, t == i >> 1)

    def score_stage(i, t, slot):
        s_ref, mx_ref, a_ref = slots[slot]
        diag_front = merged(i, t)
        lo = pl.multiple_of(t * tk, tk)
        hi = pl.multiple_of(jnp.where(diag_front, half, t * tk + half), half)
        kt = jnp.concatenate([k_ref[pl.ds(lo, half), :], k_ref[pl.ds(hi, half), :]], axis=0)
        kind = jnp.where(diag_front, _DIAG_AND_FRONT,
                         jnp.where(t == 0, _FRONT_KEYS, jnp.where(t == i >> 1, _DIAG, _PLAIN)))
        s = _dot(jnp.concatenate([kt, tbl_ref[kind]], axis=1),
                 qst_ref[i & (ATTN_TILE_BUFS - 1)])
        s_ref[...] = s
        first = t == (i & 1) ^ 1
        m_prev = jnp.where(first, NEG_INF, m_ref[...])
        m_new = jnp.maximum(m_prev, jnp.max(s, axis=0, keepdims=True))
        mx_ref[...] = m_new
        a_ref[...] = jnp.exp2(m_prev - m_new)
        m_ref[...] = m_new

    def softmax_value_stage(i, t, slot):
        s_ref, mx_ref, a_ref = slots[slot]
        p = jnp.exp2(s_ref[...] - mx_ref[...]).astype(BF16)
        t_hi = jnp.where(merged(i, t), 0, t)
        v_aug = jnp.concatenate(
            [jnp.concatenate([vt_ref[t][:, 0:half], vt_ref[t_hi][:, half:]], axis=1), ones_rows],
            axis=0)
        par = i & (ATTN_TILE_BUFS - 1)
        acc_ref[par] = a_ref[...] * acc_ref[par] + _dot(v_aug, p)

    lam = (jnp.exp(jnp.sum(lq1_ref[...] * lk1_ref[...], axis=-1, keepdims=True))
           - jnp.exp(jnp.sum(lq2_ref[...] * lk2_ref[...], axis=-1, keepdims=True))
           + LAMBDA_INIT)

    def finalize(i):
        par = i & (ATTN_TILE_BUFS - 1)
        l = acc_ref[par, V_HEAD_DIM:V_HEAD_DIM + 1, :]
        acc = acc_ref[par, 0:V_HEAD_DIM, :]
        o = acc[:, 0:tq] / l[:, 0:tq] - lam * (acc[:, tq:] / l[:, tq:])
        o = o * lax.rsqrt(jnp.mean(o * o, axis=0, keepdims=True) + 1e-6)
        y = o.T * sw_ref[...] * (1.0 - LAMBDA_INIT)
        o_ref[pl.ds(pl.multiple_of(i * tq, tq), tq), :] = y.astype(BF16)

    def advance(i, t):
        wrap = t == i >> 1
        return jnp.where(wrap, i + 1, i), jnp.where(wrap, i & 1, t + 1)

    def build_if_first(i, t):
        @pl.when(jnp.logical_and(t == (i & 1) ^ 1, i < n_qt))
        def _():
            build_queries(i)

    def finalize_if_last(i, t):
        @pl.when(t == i >> 1)
        def _():
            finalize(i)

    def following(step, count):
        steps = [step]
        for _ in range(count - 1):
            steps.append(advance(*steps[-1]))
        return steps[:count]

    def block(first, base):
        steps = following(first, 2 * w)
        for step in steps[w:]:
            build_if_first(*step)
        for j in range(w):
            score_stage(*steps[w + j], (base + w + j) % n_slots)
            softmax_value_stage(*steps[j], base + j)
        for step in steps[:w]:
            finalize_if_last(*step)
        return steps[w]

    start = (jnp.int32(1), jnp.int32(0))
    for j, step in enumerate(following(start, w)):
        build_if_first(*step)
        score_stage(*step, j)
    n_trips = n_steps // n_slots
    first = lax.fori_loop(0, n_trips, lambda _, c: block(block(c, 0), w), start)
    for j, step in enumerate(following(first, n_steps - n_trips * n_slots)):
        if j >= w:
            build_if_first(*step)
            score_stage(*step, j)
        softmax_value_stage(*step, j)
        finalize_if_last(*step)


def _attention(qt, k, vt, lq1, lk1, lq2, lk2, subln_w):
    b, l, d = k.shape
    n_qt = l // TQ
    col = pl.BlockSpec((None, l, V_HEAD_DIM), lambda bi, h: (bi, 0, h))
    qtspec = pl.BlockSpec((None, n_qt, V_HEAD_DIM, TQ), lambda bi, h: (bi, 0, h, 0))
    vtspec = pl.BlockSpec((None, l // TK, V_HEAD_DIM, TK), lambda bi, h: (bi, 0, h, 0))
    lspec = _const_spec((1, HEAD_DIM))
    nq = 2 * TQ
    per_slot = lambda shape, dtype: [pltpu.VMEM(shape, dtype)] * (2 * ATTN_W)
    return pl.pallas_call(
        functools.partial(_attn_kernel, n_qt=n_qt),
        grid=(b, N_HEADS),
        in_specs=[qtspec, col, vtspec, lspec, lspec, lspec, lspec, _const_spec((1, V_HEAD_DIM))],
        out_specs=col,
        out_shape=jax.ShapeDtypeStruct((b, l, d), BF16),
        scratch_shapes=[
            pltpu.VMEM((ATTN_TILE_BUFS, 2 * V_HEAD_DIM, nq), BF16),
            pltpu.VMEM((4, TK, V_HEAD_DIM), BF16),
            pltpu.VMEM((TK // TQ, V_HEAD_DIM, TQ), BF16),
            pltpu.VMEM((1, nq), F32),
            pltpu.VMEM((ATTN_TILE_BUFS, V_HEAD_DIM + DENOM_ROWS, nq), F32),
            *per_slot((TK, nq), F32),
            *per_slot((1, nq), F32),
            *per_slot((1, nq), F32),
        ],
        compiler_params=_params(2),
        name="diff_attn",
    )(qt, k, vt, lq1, lk1, lq2, lk2, subln_w)


def _mix_kernel(u_ref, uh_ref, x_ref, meta_ref, ya_ref, cw_ref, cb_ref, lnw_ref, lnb_ref,
                wco_ref, nw_ref, wg_ref, wo_ref, o_ref, ubuf_ref, act_ref):
    tm, d = act_ref.shape

    ubuf_ref[0:HALO, :] = uh_ref[...]
    ubuf_ref[HALO:, :] = u_ref[...]

    h = _front_or_frames(x_ref, meta_ref, pl.program_id(1) == 0)
    n = _rms_rows(h, nw_ref[...]).astype(BF16)
    gate_att = _sigmoid(_dot(n, wg_ref[:, 0:d])) * ya_ref[...].astype(F32)
    gate_conv = _sigmoid(_dot(n, wg_ref[:, d:2 * d]))

    for base in range(0, tm, CONV_ROWS):
        win = ubuf_ref[base:base + CONV_ROWS + HALO, :]
        groups = (CONV_ROWS // SUBLANES, SUBLANES, d)
        acc = jnp.broadcast_to(cb_ref[...][None], groups)
        for c in range(SUBLANES):
            taps = [t for t in range(CONV_K) if (HALO - CONV_K + 1 + t) % SUBLANES == c]
            span = max(HALO - CONV_K + 1 + t for t in taps) - c + CONV_ROWS
            shifted = win[c:c + span, :]
            for t in taps:
                q = HALO - CONV_K + 1 + t - c
                acc = acc + cw_ref[t][None] * shifted[q:q + CONV_ROWS, :].reshape(groups)
        acc = acc.reshape(CONV_ROWS, d)
        mu = jnp.mean(acc, axis=-1, keepdims=True)
        xc = acc - mu
        y = xc * lax.rsqrt(jnp.mean(xc * xc, axis=-1, keepdims=True) + 1e-5)
        y = y * lnw_ref[...] + lnb_ref[...]
        act_ref[base:base + CONV_ROWS, :] = (y * _sigmoid(y)).astype(BF16)

    m = gate_att + gate_conv * _dot(act_ref[...], wco_ref[...])
    o_ref[...] = h + _dot(m.astype(BF16), wo_ref[...])


def _mix(u, x, meta, y_att, conv_w, conv_b, ln_w, ln_b, w_conv_out, norm_w, w_gate, w_out):
    b, l, d = u.shape
    tm = TM_ROW
    skip = SKIP // tm
    row = pl.BlockSpec((None, tm, d), lambda bi, i: (bi, i + skip, 0))
    frames = pl.BlockSpec((None, tm, d), lambda bi, i: (bi, jnp.maximum(i - 1, 0), 0))
    halo = pl.BlockSpec((None, HALO, d), lambda bi, i: (bi, (i + skip) * (tm // HALO) - 1, 0))
    vec = _const_spec((1, d))
    return pl.pallas_call(
        _mix_kernel,
        grid=(b, l // tm - skip),
        in_specs=[row, halo, frames, _const_spec(meta.shape), row, _const_spec(conv_w.shape),
                  _const_spec(conv_b.shape), vec, vec,
                  _const_spec(w_conv_out.shape), vec, _const_spec(w_gate.shape),
                  _const_spec(w_out.shape)],
        out_specs=row,
        out_shape=jax.ShapeDtypeStruct((b, l, d), F32),
        scratch_shapes=[pltpu.VMEM((tm + HALO, d), F32), pltpu.VMEM((tm, d), BF16)],
        compiler_params=_params(2),
        name="conv_mix",
    )(u, u, x, meta, y_att, conv_w, conv_b, ln_w, ln_b, w_conv_out, norm_w, w_gate, w_out)


def _ffn_kernel(h_ref, nw_ref, wup_ref, fw_ref, fb_ref, wdn_ref, fnw_ref, o_ref,
                up_ref, act_ref):
    i = pl.program_id(1)
    tm = h_ref.shape[0]
    ff = D_FF
    lc = 256

    @pl.when(i == 0)
    def _():
        up_ref[0:FFN_HALO, :] = jnp.zeros((FFN_HALO, 2 * ff), F32)

    h = h_ref[...]
    n = _rms_rows(h, nw_ref[...]).astype(BF16)
    up_ref[FFN_HALO:, :] = _dot(n, wup_ref[...])

    def conv(c0):
        acc = jnp.broadcast_to(fb_ref[:, c0:c0 + lc], (tm, lc))
        for t in range(FFN_CONV_K):
            r0 = FFN_HALO - (FFN_CONV_K - 1) + t
            acc = acc + fw_ref[t:t + 1, c0:c0 + lc] * up_ref[r0:r0 + tm, c0:c0 + lc]
        return acc

    for c in range(ff // lc):
        gate = conv(c * lc)
        val = conv(ff + c * lc)
        act_ref[:, c * lc:(c + 1) * lc] = (gate * _sigmoid(gate) * val).astype(BF16)

    up_ref[0:FFN_HALO, :] = up_ref[tm:tm + FFN_HALO, :]
    h3 = h + _dot(act_ref[...], wdn_ref[...])
    o_ref[...] = _rms_rows(h3, fnw_ref[...])


def _ffn(h2, norm_w, w_up, ffn_w, ffn_b, w_down, final_w, seq):
    b, l, d = h2.shape
    tm = TM_ROW
    skip = SKIP // tm
    front_tiles = (FRONT - SKIP) // tm
    vec = _const_spec((1, d))
    return pl.pallas_call(
        _ffn_kernel,
        grid=(b, l // tm - skip),
        in_specs=[pl.BlockSpec((None, tm, d), lambda bi, i: (bi, i + skip, 0)), vec,
                  _const_spec(w_up.shape), _const_spec(ffn_w.shape),
                  _const_spec(ffn_b.shape), _const_spec(w_down.shape), vec],
        out_specs=pl.BlockSpec((None, tm, d),
                               lambda bi, i: (bi, jnp.maximum(i - front_tiles, 0), 0)),
        out_shape=jax.ShapeDtypeStruct((b, seq, d), F32),
        scratch_shapes=[pltpu.VMEM((tm + FFN_HALO, 2 * D_FF), F32),
                        pltpu.VMEM((tm, D_FF), BF16)],
        compiler_params=_params(2),
        name="ffn",
    )(h2, norm_w, w_up, ffn_w, ffn_b, w_down, final_w)


def kernel(x, meta_tokens, norm_mix_w, w_in, lambda_q1, lambda_k1, lambda_q2, lambda_k2,
           subln_w, conv_dw_w, conv_dw_b, conv_ln_w, conv_ln_b, w_conv_out, w_out,
           norm_ffn_w, w_up, ffn_dw_w, ffn_dw_b, w_down, norm_final_w):
    b, s, d = x.shape
    assert d == D_MODEL and w_in.shape[0] == 1, "single-layer trunk"
    assert s % TK == 0 and FRONT == TK == 2 * TQ and SKIP == TM_ROW and FRONT - SKIP == TM_ROW
    assert SKIP + HALO <= FRONT - N_META and TK // CHUNK <= V_HEAD_DIM and FRONT == TM_IN
    meta = meta_tokens.astype(F32)

    qkvu_cols = 3 * d + 2 * d
    w_qkvu = w_in[0, :, :qkvu_cols].astype(BF16)
    w_gate = w_in[0, :, qkvu_cols:].astype(BF16)
    vec = lambda a: a.reshape(1, -1).astype(F32)

    qt, k, vt, u = _in_proj(x, meta, vec(norm_mix_w[0]), w_qkvu)
    y_att = _attention(qt, k, vt, vec(lambda_q1[0]), vec(lambda_k1[0]), vec(lambda_q2[0]),
                       vec(lambda_k2[0]), vec(subln_w[0]))
    tile_rows = lambda a: jnp.broadcast_to(a.astype(F32)[..., None, :], a.shape[:-1] + (SUBLANES, d))
    h2 = _mix(u, x, meta, y_att, tile_rows(conv_dw_w[0]), tile_rows(conv_dw_b[0]), vec(conv_ln_w[0]),
              vec(conv_ln_b[0]), w_conv_out[0].astype(BF16), vec(norm_mix_w[0]), w_gate,
              w_out[0].astype(BF16))
    return _ffn(h2, vec(norm_ffn_w[0]), w_up[0].astype(BF16), ffn_dw_w[0].astype(F32),
                vec(ffn_dw_b[0]), w_down[0].astype(BF16), vec(norm_final_w), s)
```

```python
import functools
import math

import jax
import jax.numpy as jnp
from jax import lax
from jax.experimental import pallas as pl
from jax.experimental.pallas import tpu as pltpu

F32 = jnp.float32
BF16 = jnp.bfloat16

D_MODEL = 1024
N_META = 16
CHUNK = 64
N_HEADS = 8
HEAD_DIM = 64
V_HEAD_DIM = 2 * HEAD_DIM
CONV_K = 31
D_FF = 2816
FFN_CONV_K = 3
NEG_INF = -1e30
LAMBDA_INIT = 0.8 - 0.6 * math.exp(0.0)
LOG2_E = math.log2(math.e)

SUBLANES = 8

FRONT = 512
SKIP = 256
HALO = 32
FFN_HALO = 8
CONV_ROWS = 32
DENOM_ROWS = 16
ATTN_W = 12
ATTN_TILE_BUFS = 8

TM_IN = 512
TM_ROW = 256
TQ = 256
TK = 512
VMEM_LIMIT = 56 * 1024 * 1024


def _dot(a, b):
    return jnp.dot(a, b, preferred_element_type=F32)


def _sigmoid(x):
    return 1.0 / (1.0 + jnp.exp(-x))


def _rms_rows(x, w, eps=1e-6):
    return x * lax.rsqrt(jnp.mean(x * x, axis=-1, keepdims=True) + eps) * w


def _params(n_axes):
    return pltpu.CompilerParams(
        dimension_semantics=("arbitrary",) * n_axes, vmem_limit_bytes=VMEM_LIMIT)


def _const_spec(shape):
    return pl.BlockSpec(shape, lambda *_: (0,) * len(shape))


def _front_or_frames(x_ref, meta_ref, is_front):
    tm, d = x_ref.shape
    front = jnp.concatenate([jnp.zeros((tm - N_META, d), F32), meta_ref[...]], axis=0)
    return jnp.where(is_front, front, x_ref[...])


def _in_proj_kernel(x_ref, meta_ref, nw_ref, w_ref, qt_ref, k_ref, vt_ref, u_ref):
    d = D_MODEL
    h = _front_or_frames(x_ref, meta_ref, pl.program_id(1) == 0)
    n = _rms_rows(h, nw_ref[...]).astype(BF16)
    q = _dot(n, w_ref[:, 0:d]) * (HEAD_DIM ** -0.5 * LOG2_E)
    for t in range(qt_ref.shape[0]):
        qt_ref[t] = q[t * TQ:(t + 1) * TQ, :].T.astype(BF16)
    k_ref[...] = _dot(n, w_ref[:, d:2 * d]).astype(BF16)
    v = _dot(n, w_ref[:, 2 * d:3 * d])
    for t in range(vt_ref.shape[0]):
        vt_ref[t] = v[t * TK:(t + 1) * TK, :].T.astype(BF16)
    a = _dot(n, w_ref[:, 3 * d:4 * d])
    g = _dot(n, w_ref[:, 4 * d:5 * d])
    u_ref[...] = a * _sigmoid(g)


def _in_proj(x, meta, norm_w, w_qkvu):
    b, s, d = x.shape
    l = s + FRONT
    tm = TM_IN
    row = pl.BlockSpec((None, tm, d), lambda bi, i: (bi, i, 0))
    frames = pl.BlockSpec((None, tm, d), lambda bi, i: (bi, jnp.maximum(i - 1, 0), 0))
    qt = pl.BlockSpec((None, tm // TQ, d, TQ), lambda bi, i: (bi, i, 0, 0))
    vt = pl.BlockSpec((None, tm // TK, d, TK), lambda bi, i: (bi, i, 0, 0))
    return pl.pallas_call(
        _in_proj_kernel,
        grid=(b, l // tm),
        in_specs=[frames, _const_spec(meta.shape), _const_spec((1, d)), _const_spec(w_qkvu.shape)],
        out_specs=[qt, row, vt, row],
        out_shape=[jax.ShapeDtypeStruct((b, l // TQ, d, TQ), BF16),
                   jax.ShapeDtypeStruct((b, l, d), BF16),
                   jax.ShapeDtypeStruct((b, l // TK, d, TK), BF16),
                   jax.ShapeDtypeStruct((b, l, d), F32)],
        compiler_params=_params(2),
        name="in_proj",
    )(x, meta, norm_w, w_qkvu)


_PLAIN, _DIAG, _FRONT_KEYS, _DIAG_AND_FRONT = 0, 1, 2, 3


def _attn_steps(n_qt):
    return sum(i // 2 + i % 2 for i in range(1, n_qt))


def _attn_kernel(qt_ref, k_ref, vt_ref, lq1_ref, lk1_ref, lq2_ref, lk2_ref, o_ref,
                 qst_ref, tbl_ref, onehot_ref, m_ref, acc_ref, *slot_refs, n_qt):
    tq, tk, w = TQ, TK, ATTN_W
    n_steps = _attn_steps(n_qt)
    n_slots = 2 * w
    slots = tuple(zip(slot_refs[0:n_slots], slot_refs[n_slots:2 * n_slots], slot_refs[2 * n_slots:]))

    key = lax.broadcasted_iota(jnp.int32, (tk, V_HEAD_DIM), 0)
    lane = lax.broadcasted_iota(jnp.int32, (tk, V_HEAD_DIM), 1)
    chunk_lane = lane < tk // CHUNK

    def bias(masked):
        return jnp.where(chunk_lane & masked, NEG_INF, 0.0).astype(BF16)

    tbl_ref[_PLAIN] = jnp.zeros((tk, V_HEAD_DIM), BF16)
    tbl_ref[_DIAG] = bias(key // CHUNK > lane)
    tbl_ref[_FRONT_KEYS] = bias(key < tk - N_META)
    tbl_ref[_DIAG_AND_FRONT] = bias(
        ((key < tk // 2) & (key // CHUNK > lane)) | ((key >= tk // 2) & (key < tk - N_META)))

    dim = lax.broadcasted_iota(jnp.int32, (V_HEAD_DIM, tq), 0)
    qry = lax.broadcasted_iota(jnp.int32, (V_HEAD_DIM, tq), 1)
    for part in range(tk // tq):
        chunk = part * (tq // CHUNK) + qry // CHUNK
        onehot_ref[part] = jnp.where(dim == chunk, 1.0, 0.0).astype(BF16)

    o_ref[0] = jnp.zeros((V_HEAD_DIM, tq), BF16)
    acc_ref[...] = jnp.zeros(acc_ref.shape, F32)
    m_ref[...] = jnp.full(m_ref.shape, NEG_INF, F32)
    ones_rows = jnp.ones((DENOM_ROWS, tk), BF16)

    def build_queries(i):
        qt = qt_ref[i]
        zeros = jnp.zeros((HEAD_DIM, tq), BF16)
        onehot = onehot_ref[i & (tk // tq - 1)]
        par = i & (ATTN_TILE_BUFS - 1)
        qst_ref[par, 0:HEAD_DIM, 0:tq] = qt[0:HEAD_DIM]
        qst_ref[par, HEAD_DIM:V_HEAD_DIM, 0:tq] = zeros
        qst_ref[par, 0:HEAD_DIM, tq:] = zeros
        qst_ref[par, HEAD_DIM:V_HEAD_DIM, tq:] = qt[HEAD_DIM:]
        qst_ref[par, V_HEAD_DIM:, 0:tq] = onehot
        qst_ref[par, V_HEAD_DIM:, tq:] = onehot

    half = tk // 2

    def merged(i, t):
        return jnp.logical_and(i & 1 == 0, t == i >> 1)

    def score_stage(i, t, slot):
        s_ref, mx_ref, a_ref = slots[slot]
        diag_front = merged(i, t)
        lo = pl.multiple_of(t * tk, tk)
        hi = pl.multiple_of(jnp.where(diag_front, half, t * tk + half), half)
        kt = jnp.concatenate([k_ref[pl.ds(lo, half), :], k_ref[pl.ds(hi, half), :]], axis=0)
        kind = jnp.where(diag_front, _DIAG_AND_FRONT,
                         jnp.where(t == 0, _FRONT_KEYS, jnp.where(t == i >> 1, _DIAG, _PLAIN)))
        s = _dot(jnp.concatenate([kt, tbl_ref[kind]], axis=1),
                 qst_ref[i & (ATTN_TILE_BUFS - 1)])
        s_ref[...] = s
        first = t == (i & 1) ^ 1
        m_prev = jnp.where(first, NEG_INF, m_ref[...])
        m_new = jnp.maximum(m_prev, jnp.max(s, axis=0, keepdims=True))
        mx_ref[...] = m_new
        a_ref[...] = jnp.exp2(m_prev - m_new)
        m_ref[...] = m_new

    def softmax_value_stage(i, t, slot):
        s_ref, mx_ref, a_ref = slots[slot]
        p = jnp.exp2(s_ref[...] - mx_ref[...]).astype(BF16)
        t_hi = jnp.where(merged(i, t), 0, t)
        v_aug = jnp.concatenate(
            [jnp.concatenate([vt_ref[t][:, 0:half], vt_ref[t_hi][:, half:]], axis=1), ones_rows],
            axis=0)
        par = i & (ATTN_TILE_BUFS - 1)
        acc_ref[par] = a_ref[...] * acc_ref[par] + _dot(v_aug, p)

    lam = (jnp.exp(jnp.sum(lq1_ref[...] * lk1_ref[...], axis=-1, keepdims=True))
           - jnp.exp(jnp.sum(lq2_ref[...] * lk2_ref[...], axis=-1, keepdims=True))
           + LAMBDA_INIT)

    def finalize(i):
        par = i & (ATTN_TILE_BUFS - 1)
        l = acc_ref[par, V_HEAD_DIM:V_HEAD_DIM + 1, :]
        acc = acc_ref[par, 0:V_HEAD_DIM, :]
        o = acc[:, 0:tq] / l[:, 0:tq] - lam * (acc[:, tq:] / l[:, tq:])
        o_ref[i] = (o * lax.rsqrt(jnp.mean(o * o, axis=0, keepdims=True) + 1e-6)).astype(BF16)

    def advance(i, t):
        wrap = t == i >> 1
        return jnp.where(wrap, i + 1, i), jnp.where(wrap, i & 1, t + 1)

    def build_if_first(i, t):
        @pl.when(jnp.logical_and(t == (i & 1) ^ 1, i < n_qt))
        def _():
            build_queries(i)

    def finalize_if_last(i, t):
        @pl.when(t == i >> 1)
        def _():
            finalize(i)

    def following(step, count):
        steps = [step]
        for _ in range(count - 1):
            steps.append(advance(*steps[-1]))
        return steps[:count]

    def block(first, base):
        steps = following(first, 2 * w)
        for step in steps[w:]:
            build_if_first(*step)
        for j in range(w):
            score_stage(*steps[w + j], (base + w + j) % n_slots)
            softmax_value_stage(*steps[j], base + j)
        for step in steps[:w]:
            finalize_if_last(*step)
        return steps[w]

    start = (jnp.int32(1), jnp.int32(0))
    for j, step in enumerate(following(start, w)):
        build_if_first(*step)
        score_stage(*step, j)
    n_trips = n_steps // n_slots
    first = lax.fori_loop(0, n_trips, lambda _, c: block(block(c, 0), w), start)
    for j, step in enumerate(following(first, n_steps - n_trips * n_slots)):
        if j >= w:
            build_if_first(*step)
            score_stage(*step, j)
        softmax_value_stage(*step, j)
        finalize_if_last(*step)


def _attention(qt, k, vt, lq1, lk1, lq2, lk2):
    b, l, d = k.shape
    n_qt = l // TQ
    col = pl.BlockSpec((None, l, V_HEAD_DIM), lambda bi, h: (bi, 0, h))
    qtspec = pl.BlockSpec((None, n_qt, V_HEAD_DIM, TQ), lambda bi, h: (bi, 0, h, 0))
    vtspec = pl.BlockSpec((None, l // TK, V_HEAD_DIM, TK), lambda bi, h: (bi, 0, h, 0))
    lspec = _const_spec((1, HEAD_DIM))
    nq = 2 * TQ
    per_slot = lambda shape, dtype: [pltpu.VMEM(shape, dtype)] * (2 * ATTN_W)
    return pl.pallas_call(
        functools.partial(_attn_kernel, n_qt=n_qt),
        grid=(b, N_HEADS),
        in_specs=[qtspec, col, vtspec, lspec, lspec, lspec, lspec],
        out_specs=pl.BlockSpec((None, None, n_qt, V_HEAD_DIM, TQ), lambda bi, h: (bi, h, 0, 0, 0)),
        out_shape=jax.ShapeDtypeStruct((b, N_HEADS, n_qt, V_HEAD_DIM, TQ), BF16),
        scratch_shapes=[
            pltpu.VMEM((ATTN_TILE_BUFS, 2 * V_HEAD_DIM, nq), BF16),
            pltpu.VMEM((4, TK, V_HEAD_DIM), BF16),
            pltpu.VMEM((TK // TQ, V_HEAD_DIM, TQ), BF16),
            pltpu.VMEM((1, nq), F32),
            pltpu.VMEM((ATTN_TILE_BUFS, V_HEAD_DIM + DENOM_ROWS, nq), F32),
            *per_slot((TK, nq), F32),
            *per_slot((1, nq), F32),
            *per_slot((1, nq), F32),
        ],
        compiler_params=_params(2),
        name="diff_attn",
    )(qt, k, vt, lq1, lk1, lq2, lk2)


def _mix_kernel(u_ref, uh_ref, x_ref, meta_ref, yat_ref, sw_ref, cw_ref, cb_ref, lnw_ref,
                lnb_ref, wco_ref, nw_ref, wg_ref, wo_ref, o_ref, ubuf_ref, act_ref):
    tm, d = act_ref.shape

    ubuf_ref[0:HALO, :] = uh_ref[...]
    ubuf_ref[HALO:, :] = u_ref[...]

    h = _front_or_frames(x_ref, meta_ref, pl.program_id(1) == 0)
    n = _rms_rows(h, nw_ref[...]).astype(BF16)
    y_att = jnp.concatenate([yat_ref[hd].astype(F32).T for hd in range(N_HEADS)], axis=1)
    gate_att = _sigmoid(_dot(n, wg_ref[:, 0:d])) * (y_att * sw_ref[...] * (1.0 - LAMBDA_INIT))
    gate_conv = _sigmoid(_dot(n, wg_ref[:, d:2 * d]))

    for base in range(0, tm, CONV_ROWS):
        win = ubuf_ref[base:base + CONV_ROWS + HALO, :]
        groups = (CONV_ROWS // SUBLANES, SUBLANES, d)
        acc = jnp.broadcast_to(cb_ref[...][None], groups)
        for c in range(SUBLANES):
            taps = [t for t in range(CONV_K) if (HALO - CONV_K + 1 + t) % SUBLANES == c]
            span = max(HALO - CONV_K + 1 + t for t in taps) - c + CONV_ROWS
            shifted = win[c:c + span, :]
            for t in taps:
                q = HALO - CONV_K + 1 + t - c
                acc = acc + cw_ref[t][None] * shifted[q:q + CONV_ROWS, :].reshape(groups)
        acc = acc.reshape(CONV_ROWS, d)
        mu = jnp.mean(acc, axis=-1, keepdims=True)
        xc = acc - mu
        y = xc * lax.rsqrt(jnp.mean(xc * xc, axis=-1, keepdims=True) + 1e-5)
        y = y * lnw_ref[...] + lnb_ref[...]
        act_ref[base:base + CONV_ROWS, :] = (y * _sigmoid(y)).astype(BF16)

    m = gate_att + gate_conv * _dot(act_ref[...], wco_ref[...])
    o_ref[...] = h + _dot(m.astype(BF16), wo_ref[...])


def _mix(u, x, meta, y_att_t, subln_w, conv_w, conv_b, ln_w, ln_b, w_conv_out, norm_w, w_gate,
         w_out):
    b, l, d = u.shape
    tm = TM_ROW
    skip = SKIP // tm
    row = pl.BlockSpec((None, tm, d), lambda bi, i: (bi, i + skip, 0))
    frames = pl.BlockSpec((None, tm, d), lambda bi, i: (bi, jnp.maximum(i - 1, 0), 0))
    halo = pl.BlockSpec((None, HALO, d), lambda bi, i: (bi, (i + skip) * (tm // HALO) - 1, 0))
    vec = _const_spec((1, d))
    assert tm == TQ
    heads = pl.BlockSpec((None, N_HEADS, None, V_HEAD_DIM, TQ), lambda bi, i: (bi, 0, i + skip, 0, 0))
    return pl.pallas_call(
        _mix_kernel,
        grid=(b, l // tm - skip),
        in_specs=[row, halo, frames, _const_spec(meta.shape), heads, vec, _const_spec(conv_w.shape),
                  _const_spec(conv_b.shape), vec, vec,
                  _const_spec(w_conv_out.shape), vec, _const_spec(w_gate.shape),
                  _const_spec(w_out.shape)],
        out_specs=row,
        out_shape=jax.ShapeDtypeStruct((b, l, d), F32),
        scratch_shapes=[pltpu.VMEM((tm + HALO, d), F32), pltpu.VMEM((tm, d), BF16)],
        compiler_params=_params(2),
        name="conv_mix",
    )(u, u, x, meta, y_att_t, subln_w, conv_w, conv_b, ln_w, ln_b, w_conv_out, norm_w, w_gate,
      w_out)


def _ffn_kernel(h_ref, nw_ref, wup_ref, fw_ref, fb_ref, wdn_ref, fnw_ref, o_ref,
                up_ref, act_ref):
    i = pl.program_id(1)
    tm = h_ref.shape[0]
    ff = D_FF
    lc = 256

    @pl.when(i == 0)
    def _():
        up_ref[0:FFN_HALO, :] = jnp.zeros((FFN_HALO, 2 * ff), F32)

    h = h_ref[...]
    n = _rms_rows(h, nw_ref[...]).astype(BF16)
    up_ref[FFN_HALO:, :] = _dot(n, wup_ref[...])

    def conv(c0):
        acc = jnp.broadcast_to(fb_ref[:, c0:c0 + lc], (tm, lc))
        for t in range(FFN_CONV_K):
            r0 = FFN_HALO - (FFN_CONV_K - 1) + t
            acc = acc + fw_ref[t:t + 1, c0:c0 + lc] * up_ref[r0:r0 + tm, c0:c0 + lc]
        return acc

    for c in range(ff // lc):
        gate = conv(c * lc)
        val = conv(ff + c * lc)
        act_ref[:, c * lc:(c + 1) * lc] = (gate * _sigmoid(gate) * val).astype(BF16)

    up_ref[0:FFN_HALO, :] = up_ref[tm:tm + FFN_HALO, :]
    h3 = h + _dot(act_ref[...], wdn_ref[...])
    o_ref[...] = _rms_rows(h3, fnw_ref[...])


def _ffn(h2, norm_w, w_up, ffn_w, ffn_b, w_down, final_w, seq):
    b, l, d = h2.shape
    tm = TM_ROW
    skip = SKIP // tm
    front_tiles = (FRONT - SKIP) // tm
    vec = _const_spec((1, d))
    return pl.pallas_call(
        _ffn_kernel,
        grid=(b, l // tm - skip),
        in_specs=[pl.BlockSpec((None, tm, d), lambda bi, i: (bi, i + skip, 0)), vec,
                  _const_spec(w_up.shape), _const_spec(ffn_w.shape),
                  _const_spec(ffn_b.shape), _const_spec(w_down.shape), vec],
        out_specs=pl.BlockSpec((None, tm, d),
                               lambda bi, i: (bi, jnp.maximum(i - front_tiles, 0), 0)),
        out_shape=jax.ShapeDtypeStruct((b, seq, d), F32),
        scratch_shapes=[pltpu.VMEM((tm + FFN_HALO, 2 * D_FF), F32),
                        pltpu.VMEM((tm, D_FF), BF16)],
        compiler_params=_params(2),
        name="ffn",
    )(h2, norm_w, w_up, ffn_w, ffn_b, w_down, final_w)


def kernel(x, meta_tokens, norm_mix_w, w_in, lambda_q1, lambda_k1, lambda_q2, lambda_k2,
           subln_w, conv_dw_w, conv_dw_b, conv_ln_w, conv_ln_b, w_conv_out, w_out,
           norm_ffn_w, w_up, ffn_dw_w, ffn_dw_b, w_down, norm_final_w):
    b, s, d = x.shape
    assert d == D_MODEL and w_in.shape[0] == 1, "single-layer trunk"
    assert s % TK == 0 and FRONT == TK == 2 * TQ and SKIP == TM_ROW and FRONT - SKIP == TM_ROW
    assert SKIP + HALO <= FRONT - N_META and TK // CHUNK <= V_HEAD_DIM and FRONT == TM_IN
    meta = meta_tokens.astype(F32)

    qkvu_cols = 3 * d + 2 * d
    w_qkvu = w_in[0, :, :qkvu_cols].astype(BF16)
    w_gate = w_in[0, :, qkvu_cols:].astype(BF16)
    vec = lambda a: a.reshape(1, -1).astype(F32)

    qt, k, vt, u = _in_proj(x, meta, vec(norm_mix_w[0]), w_qkvu)
    y_att_t = _attention(qt, k, vt, vec(lambda_q1[0]), vec(lambda_k1[0]), vec(lambda_q2[0]),
                         vec(lambda_k2[0]))
    tile_rows = lambda a: jnp.broadcast_to(a.astype(F32)[..., None, :], a.shape[:-1] + (SUBLANES, d))
    h2 = _mix(u, x, meta, y_att_t, vec(jnp.tile(subln_w[0], N_HEADS)), tile_rows(conv_dw_w[0]),
              tile_rows(conv_dw_b[0]), vec(conv_ln_w[0]), vec(conv_ln_b[0]),
              w_conv_out[0].astype(BF16), vec(norm_mix_w[0]), w_gate, w_out[0].astype(BF16))
    return _ffn(h2, vec(norm_ffn_w[0]), w_up[0].astype(BF16), ffn_dw_w[0].astype(F32),
                vec(ffn_dw_b[0]), w_down[0].astype(BF16), vec(norm_final_w), s)
```

```python
import functools
import math

import jax
import jax.numpy as jnp
from jax import lax
from jax.experimental import pallas as pl
from jax.experimental.pallas import tpu as pltpu

F32 = jnp.float32
BF16 = jnp.bfloat16

D_MODEL = 1024
N_META = 16
CHUNK = 64
N_HEADS = 8
HEAD_DIM = 64
V_HEAD_DIM = 2 * HEAD_DIM
CONV_K = 31
D_FF = 2816
FFN_CONV_K = 3
NEG_INF = -1e30
LAMBDA_INIT = 0.8 - 0.6 * math.exp(0.0)
LOG2_E = math.log2(math.e)

SUBLANES = 8

FRONT = 512
SKIP = 256
HALO = 32
FFN_HALO = 8
CONV_ROWS = 32
DENOM_ROWS = 16
ATTN_W = 12
ATTN_TILE_BUFS = 8

TM_IN = 512
TM_ROW = 256
TQ = 256
TK = 512
VMEM_LIMIT = 56 * 1024 * 1024


def _dot(a, b):
    return jnp.dot(a, b, preferred_element_type=F32)


def _sigmoid(x):
    return 1.0 / (1.0 + jnp.exp(-x))


def _rms_rows(x, w, eps=1e-6):
    return x * lax.rsqrt(jnp.mean(x * x, axis=-1, keepdims=True) + eps) * w


def _params(n_axes):
    return pltpu.CompilerParams(
        dimension_semantics=("arbitrary",) * n_axes, vmem_limit_bytes=VMEM_LIMIT)


def _const_spec(shape):
    return pl.BlockSpec(shape, lambda *_: (0,) * len(shape))


def _front_or_frames(x_ref, meta_ref, is_front):
    tm, d = x_ref.shape
    front = jnp.concatenate([jnp.zeros((tm - N_META, d), F32), meta_ref[...]], axis=0)
    return jnp.where(is_front, front, x_ref[...])


def _in_proj_kernel(x_ref, meta_ref, nw_ref, w_ref, qt_ref, k_ref, vt_ref, u_ref):
    d = D_MODEL
    h = _front_or_frames(x_ref, meta_ref, pl.program_id(1) == 0)
    n = _rms_rows(h, nw_ref[...]).astype(BF16)
    q = _dot(n, w_ref[:, 0:d]) * (HEAD_DIM ** -0.5 * LOG2_E)
    for t in range(qt_ref.shape[0]):
        qt_ref[t] = q[t * TQ:(t + 1) * TQ, :].T.astype(BF16)
    k_ref[...] = _dot(n, w_ref[:, d:2 * d]).astype(BF16)
    v = _dot(n, w_ref[:, 2 * d:3 * d])
    for t in range(vt_ref.shape[0]):
        vt_ref[t] = v[t * TK:(t + 1) * TK, :].T.astype(BF16)
    a = _dot(n, w_ref[:, 3 * d:4 * d])
    g = _dot(n, w_ref[:, 4 * d:5 * d])
    u_ref[...] = a * _sigmoid(g)


def _in_proj(x, meta, norm_w, w_in):
    b, s, d = x.shape
    l = s + FRONT
    tm = TM_IN
    row = pl.BlockSpec((None, tm, d), lambda bi, i: (bi, i, 0))
    frames = pl.BlockSpec((None, tm, d), lambda bi, i: (bi, jnp.maximum(i - 1, 0), 0))
    qt = pl.BlockSpec((None, tm // TQ, d, TQ), lambda bi, i: (bi, i, 0, 0))
    vt = pl.BlockSpec((None, tm // TK, d, TK), lambda bi, i: (bi, i, 0, 0))
    return pl.pallas_call(
        _in_proj_kernel,
        grid=(b, l // tm),
        in_specs=[frames, _const_spec(meta.shape), _const_spec((1, d)), _const_spec((d, 5 * d))],
        out_specs=[qt, row, vt, row],
        out_shape=[jax.ShapeDtypeStruct((b, l // TQ, d, TQ), BF16),
                   jax.ShapeDtypeStruct((b, l, d), BF16),
                   jax.ShapeDtypeStruct((b, l // TK, d, TK), BF16),
                   jax.ShapeDtypeStruct((b, l, d), F32)],
        compiler_params=_params(2),
        name="in_proj",
    )(x, meta, norm_w, w_in)


_PLAIN, _DIAG, _FRONT_KEYS, _DIAG_AND_FRONT = 0, 1, 2, 3


def _attn_steps(n_qt):
    return sum(i // 2 + i % 2 for i in range(1, n_qt))


def _attn_kernel(qt_ref, k_ref, vt_ref, lq1_ref, lk1_ref, lq2_ref, lk2_ref, o_ref,
                 qst_ref, tbl_ref, onehot_ref, m_ref, acc_ref, *slot_refs, n_qt):
    tq, tk, w = TQ, TK, ATTN_W
    n_steps = _attn_steps(n_qt)
    n_slots = 2 * w
    slots = tuple(zip(slot_refs[0:n_slots], slot_refs[n_slots:2 * n_slots], slot_refs[2 * n_slots:]))

    key = lax.broadcasted_iota(jnp.int32, (tk, V_HEAD_DIM), 0)
    lane = lax.broadcasted_iota(jnp.int32, (tk, V_HEAD_DIM), 1)
    chunk_lane = lane < tk // CHUNK

    def bias(masked):
        return jnp.where(chunk_lane & masked, NEG_INF, 0.0).astype(BF16)

    tbl_ref[_PLAIN] = jnp.zeros((tk, V_HEAD_DIM), BF16)
    tbl_ref[_DIAG] = bias(key // CHUNK > lane)
    tbl_ref[_FRONT_KEYS] = bias(key < tk - N_META)
    tbl_ref[_DIAG_AND_FRONT] = bias(
        ((key < tk // 2) & (key // CHUNK > lane)) | ((key >= tk // 2) & (key < tk - N_META)))

    dim = lax.broadcasted_iota(jnp.int32, (V_HEAD_DIM, tq), 0)
    qry = lax.broadcasted_iota(jnp.int32, (V_HEAD_DIM, tq), 1)
    for part in range(tk // tq):
        chunk = part * (tq // CHUNK) + qry // CHUNK
        onehot_ref[part] = jnp.where(dim == chunk, 1.0, 0.0).astype(BF16)

    o_ref[0] = jnp.zeros((V_HEAD_DIM, tq), BF16)
    acc_ref[...] = jnp.zeros(acc_ref.shape, F32)
    m_ref[...] = jnp.full(m_ref.shape, NEG_INF, F32)
    ones_rows = jnp.ones((DENOM_ROWS, tk), BF16)

    def build_queries(i):
        qt = qt_ref[i]
        zeros = jnp.zeros((HEAD_DIM, tq), BF16)
        onehot = onehot_ref[i & (tk // tq - 1)]
        par = i & (ATTN_TILE_BUFS - 1)
        qst_ref[par, 0:HEAD_DIM, 0:tq] = qt[0:HEAD_DIM]
        qst_ref[par, HEAD_DIM:V_HEAD_DIM, 0:tq] = zeros
        qst_ref[par, 0:HEAD_DIM, tq:] = zeros
        qst_ref[par, HEAD_DIM:V_HEAD_DIM, tq:] = qt[HEAD_DIM:]
        qst_ref[par, V_HEAD_DIM:, 0:tq] = onehot
        qst_ref[par, V_HEAD_DIM:, tq:] = onehot

    half = tk // 2

    def merged(i, t):
        return jnp.logical_and(i & 1 == 0, t == i >> 1)

    def score_stage(i, t, slot):
        s_ref, mx_ref, a_ref = slots[slot]
        diag_front = merged(i, t)
        lo = pl.multiple_of(t * tk, tk)
        hi = pl.multiple_of(jnp.where(diag_front, half, t * tk + half), half)
        kt = jnp.concatenate([k_ref[pl.ds(lo, half), :], k_ref[pl.ds(hi, half), :]], axis=0)
        kind = jnp.where(diag_front, _DIAG_AND_FRONT,
                         jnp.where(t == 0, _FRONT_KEYS, jnp.where(t == i >> 1, _DIAG, _PLAIN)))
        s = _dot(jnp.concatenate([kt, tbl_ref[kind]], axis=1),
                 qst_ref[i & (ATTN_TILE_BUFS - 1)])
        s_ref[...] = s
        first = t == (i & 1) ^ 1
        m_prev = jnp.where(first, NEG_INF, m_ref[...])
        m_new = jnp.maximum(m_prev, jnp.max(s, axis=0, keepdims=True))
        mx_ref[...] = m_new
        a_ref[...] = jnp.exp2(m_prev - m_new)
        m_ref[...] = m_new

    def softmax_value_stage(i, t, slot):
        s_ref, mx_ref, a_ref = slots[slot]
        p = jnp.exp2(s_ref[...] - mx_ref[...]).astype(BF16)
        t_hi = jnp.where(merged(i, t), 0, t)
        v_aug = jnp.concatenate(
            [jnp.concatenate([vt_ref[t][:, 0:half], vt_ref[t_hi][:, half:]], axis=1), ones_rows],
            axis=0)
        par = i & (ATTN_TILE_BUFS - 1)
        acc_ref[par] = a_ref[...] * acc_ref[par] + _dot(v_aug, p)

    lam = (jnp.exp(jnp.sum(lq1_ref[...] * lk1_ref[...], axis=-1, keepdims=True))
           - jnp.exp(jnp.sum(lq2_ref[...] * lk2_ref[...], axis=-1, keepdims=True))
           + LAMBDA_INIT)

    def finalize(i):
        par = i & (ATTN_TILE_BUFS - 1)
        l = acc_ref[par, V_HEAD_DIM:V_HEAD_DIM + 1, :]
        acc = acc_ref[par, 0:V_HEAD_DIM, :]
        o = acc[:, 0:tq] / l[:, 0:tq] - lam * (acc[:, tq:] / l[:, tq:])
        o_ref[i] = (o * lax.rsqrt(jnp.mean(o * o, axis=0, keepdims=True) + 1e-6)).astype(BF16)

    def advance(i, t):
        wrap = t == i >> 1
        return jnp.where(wrap, i + 1, i), jnp.where(wrap, i & 1, t + 1)

    def build_if_first(i, t):
        @pl.when(jnp.logical_and(t == (i & 1) ^ 1, i < n_qt))
        def _():
            build_queries(i)

    def finalize_if_last(i, t):
        @pl.when(t == i >> 1)
        def _():
            finalize(i)

    def following(step, count):
        steps = [step]
        for _ in range(count - 1):
            steps.append(advance(*steps[-1]))
        return steps[:count]

    def block(first, base):
        steps = following(first, 2 * w)
        for step in steps[w:]:
            build_if_first(*step)
        for j in range(w):
            score_stage(*steps[w + j], (base + w + j) % n_slots)
            softmax_value_stage(*steps[j], base + j)
        for step in steps[:w]:
            finalize_if_last(*step)
        return steps[w]

    start = (jnp.int32(1), jnp.int32(0))
    for j, step in enumerate(following(start, w)):
        build_if_first(*step)
        score_stage(*step, j)
    n_trips = n_steps // n_slots
    first = lax.fori_loop(0, n_trips, lambda _, c: block(block(c, 0), w), start)
    for j, step in enumerate(following(first, n_steps - n_trips * n_slots)):
        if j >= w:
            build_if_first(*step)
            score_stage(*step, j)
        softmax_value_stage(*step, j)
        finalize_if_last(*step)


def _attention(qt, k, vt, lq1, lk1, lq2, lk2):
    b, l, d = k.shape
    n_qt = l // TQ
    col = pl.BlockSpec((None, l, V_HEAD_DIM), lambda bi, h: (bi, 0, h))
    qtspec = pl.BlockSpec((None, n_qt, V_HEAD_DIM, TQ), lambda bi, h: (bi, 0, h, 0))
    vtspec = pl.BlockSpec((None, l // TK, V_HEAD_DIM, TK), lambda bi, h: (bi, 0, h, 0))
    lspec = _const_spec((1, HEAD_DIM))
    nq = 2 * TQ
    per_slot = lambda shape, dtype: [pltpu.VMEM(shape, dtype)] * (2 * ATTN_W)
    return pl.pallas_call(
        functools.partial(_attn_kernel, n_qt=n_qt),
        grid=(b, N_HEADS),
        in_specs=[qtspec, col, vtspec, lspec, lspec, lspec, lspec],
        out_specs=pl.BlockSpec((None, None, n_qt, V_HEAD_DIM, TQ), lambda bi, h: (bi, h, 0, 0, 0)),
        out_shape=jax.ShapeDtypeStruct((b, N_HEADS, n_qt, V_HEAD_DIM, TQ), BF16),
        scratch_shapes=[
            pltpu.VMEM((ATTN_TILE_BUFS, 2 * V_HEAD_DIM, nq), BF16),
            pltpu.VMEM((4, TK, V_HEAD_DIM), BF16),
            pltpu.VMEM((TK // TQ, V_HEAD_DIM, TQ), BF16),
            pltpu.VMEM((1, nq), F32),
            pltpu.VMEM((ATTN_TILE_BUFS, V_HEAD_DIM + DENOM_ROWS, nq), F32),
            *per_slot((TK, nq), F32),
            *per_slot((1, nq), F32),
            *per_slot((1, nq), F32),
        ],
        compiler_params=_params(2),
        name="diff_attn",
    )(qt, k, vt, lq1, lk1, lq2, lk2)


def _mix_kernel(u_ref, uh_ref, x_ref, meta_ref, yat_ref, sw_ref, cw_ref, cb_ref, lnw_ref,
                lnb_ref, wco_ref, nw_ref, wga_ref, wgc_ref, wo_ref, o_ref, ubuf_ref, act_ref):
    tm, d = act_ref.shape

    ubuf_ref[0:HALO, :] = uh_ref[...]
    ubuf_ref[HALO:, :] = u_ref[...]

    h = _front_or_frames(x_ref, meta_ref, pl.program_id(1) == 0)
    n = _rms_rows(h, nw_ref[...]).astype(BF16)
    y_att = jnp.concatenate([yat_ref[hd].astype(F32).T for hd in range(N_HEADS)], axis=1)
    gate_att = _sigmoid(_dot(n, wga_ref[...])) * (y_att * sw_ref[...])
    gate_conv = _sigmoid(_dot(n, wgc_ref[...]))

    for base in range(0, tm, CONV_ROWS):
        win = ubuf_ref[base:base + CONV_ROWS + HALO, :]
        groups = (CONV_ROWS // SUBLANES, SUBLANES, d)
        acc = jnp.broadcast_to(cb_ref[...][None], groups)
        for c in range(SUBLANES):
            taps = [t for t in range(CONV_K) if (HALO - CONV_K + 1 + t) % SUBLANES == c]
            span = max(HALO - CONV_K + 1 + t for t in taps) - c + CONV_ROWS
            shifted = win[c:c + span, :]
            for t in taps:
                q = HALO - CONV_K + 1 + t - c
                acc = acc + cw_ref[t][None] * shifted[q:q + CONV_ROWS, :].reshape(groups)
        acc = acc.reshape(CONV_ROWS, d)
        mu = jnp.mean(acc, axis=-1, keepdims=True)
        xc = acc - mu
        y = xc * lax.rsqrt(jnp.mean(xc * xc, axis=-1, keepdims=True) + 1e-5)
        y = y * lnw_ref[...] + lnb_ref[...]
        act_ref[base:base + CONV_ROWS, :] = (y * _sigmoid(y)).astype(BF16)

    m = gate_att + gate_conv * _dot(act_ref[...], wco_ref[...])
    o_ref[...] = h + _dot(m.astype(BF16), wo_ref[...])


def _mix(u, x, meta, y_att_t, subln_w, conv_w, conv_b, ln_w, ln_b, w_conv_out, norm_w, w_in,
         w_out):
    b, l, d = u.shape
    tm = TM_ROW
    skip = SKIP // tm
    row = pl.BlockSpec((None, tm, d), lambda bi, i: (bi, i + skip, 0))
    frames = pl.BlockSpec((None, tm, d), lambda bi, i: (bi, jnp.maximum(i - 1, 0), 0))
    halo = pl.BlockSpec((None, HALO, d), lambda bi, i: (bi, (i + skip) * (tm // HALO) - 1, 0))
    vec = _const_spec((1, d))
    assert tm == TQ
    heads = pl.BlockSpec((None, N_HEADS, None, V_HEAD_DIM, TQ), lambda bi, i: (bi, 0, i + skip, 0, 0))
    return pl.pallas_call(
        _mix_kernel,
        grid=(b, l // tm - skip),
        in_specs=[row, halo, frames, _const_spec(meta.shape), heads, vec, _const_spec(conv_w.shape),
                  _const_spec(conv_b.shape), vec, vec,
                  _const_spec(w_conv_out.shape), vec,
                  pl.BlockSpec((d, d), lambda *_: (0, 5)), pl.BlockSpec((d, d), lambda *_: (0, 6)),
                  _const_spec(w_out.shape)],
        out_specs=row,
        out_shape=jax.ShapeDtypeStruct((b, l, d), F32),
        scratch_shapes=[pltpu.VMEM((tm + HALO, d), F32), pltpu.VMEM((tm, d), BF16)],
        compiler_params=_params(2),
        name="conv_mix",
    )(u, u, x, meta, y_att_t, subln_w, conv_w, conv_b, ln_w, ln_b, w_conv_out, norm_w, w_in,
      w_in, w_out)


def _ffn_kernel(h_ref, nw_ref, wup_ref, fw_ref, fb_ref, wdn_ref, fnw_ref, o_ref,
                up_ref, act_ref):
    i = pl.program_id(1)
    tm = h_ref.shape[0]
    ff = D_FF
    lc = 256

    @pl.when(i == 0)
    def _():
        up_ref[0:FFN_HALO, :] = jnp.zeros((FFN_HALO, 2 * ff), F32)

    h = h_ref[...]
    n = _rms_rows(h, nw_ref[...]).astype(BF16)
    up_ref[FFN_HALO:, :] = _dot(n, wup_ref[...])

    def conv(c0):
        acc = jnp.broadcast_to(fb_ref[:, c0:c0 + lc], (tm, lc))
        for t in range(FFN_CONV_K):
            r0 = FFN_HALO - (FFN_CONV_K - 1) + t
            acc = acc + fw_ref[t:t + 1, c0:c0 + lc] * up_ref[r0:r0 + tm, c0:c0 + lc]
        return acc

    for c in range(ff // lc):
        gate = conv(c * lc)
        val = conv(ff + c * lc)
        act_ref[:, c * lc:(c + 1) * lc] = (gate * _sigmoid(gate) * val).astype(BF16)

    up_ref[0:FFN_HALO, :] = up_ref[tm:tm + FFN_HALO, :]
    h3 = h + _dot(act_ref[...], wdn_ref[...])
    o_ref[...] = _rms_rows(h3, fnw_ref[...])


def _ffn(h2, norm_w, w_up, ffn_w, ffn_b, w_down, final_w, seq):
    b, l, d = h2.shape
    tm = TM_ROW
    skip = SKIP // tm
    front_tiles = (FRONT - SKIP) // tm
    vec = _const_spec((1, d))
    return pl.pallas_call(
        _ffn_kernel,
        grid=(b, l // tm - skip),
        in_specs=[pl.BlockSpec((None, tm, d), lambda bi, i: (bi, i + skip, 0)), vec,
                  _const_spec(w_up.shape), _const_spec(ffn_w.shape),
                  _const_spec(ffn_b.shape), _const_spec(w_down.shape), vec],
        out_specs=pl.BlockSpec((None, tm, d),
                               lambda bi, i: (bi, jnp.maximum(i - front_tiles, 0), 0)),
        out_shape=jax.ShapeDtypeStruct((b, seq, d), F32),
        scratch_shapes=[pltpu.VMEM((tm + FFN_HALO, 2 * D_FF), F32),
                        pltpu.VMEM((tm, D_FF), BF16)],
        compiler_params=_params(2),
        name="ffn",
    )(h2, norm_w, w_up, ffn_w, ffn_b, w_down, final_w)


def kernel(x, meta_tokens, norm_mix_w, w_in, lambda_q1, lambda_k1, lambda_q2, lambda_k2,
           subln_w, conv_dw_w, conv_dw_b, conv_ln_w, conv_ln_b, w_conv_out, w_out,
           norm_ffn_w, w_up, ffn_dw_w, ffn_dw_b, w_down, norm_final_w):
    b, s, d = x.shape
    assert d == D_MODEL and w_in.shape[0] == 1, "single-layer trunk"
    assert s % TK == 0 and FRONT == TK == 2 * TQ and SKIP == TM_ROW and FRONT - SKIP == TM_ROW
    assert SKIP + HALO <= FRONT - N_META and TK // CHUNK <= V_HEAD_DIM and FRONT == TM_IN
    meta = meta_tokens.astype(F32)

    w_in_bf = w_in[0].astype(BF16)
    vec = lambda a: a.reshape(1, -1).astype(F32)

    qt, k, vt, u = _in_proj(x, meta, vec(norm_mix_w[0]), w_in_bf)
    y_att_t = _attention(qt, k, vt, vec(lambda_q1[0]), vec(lambda_k1[0]), vec(lambda_q2[0]),
                         vec(lambda_k2[0]))
    tile_rows = lambda a: jnp.broadcast_to(a.astype(F32)[..., None, :], a.shape[:-1] + (SUBLANES, d))
    subln_gain = vec(jnp.tile(subln_w[0], N_HEADS)) * (1.0 - LAMBDA_INIT)
    h2 = _mix(u, x, meta, y_att_t, subln_gain, tile_rows(conv_dw_w[0]),
              tile_rows(conv_dw_b[0]), vec(conv_ln_w[0]), vec(conv_ln_b[0]),
              w_conv_out[0].astype(BF16), vec(norm_mix_w[0]), w_in_bf, w_out[0].astype(BF16))
    return _ffn(h2, vec(norm_ffn_w[0]), w_up[0].astype(BF16), ffn_dw_w[0].astype(F32),
                vec(ffn_dw_b[0]), w_down[0].astype(BF16), vec(norm_final_w), s)
```

```python
import functools
import math

import jax
import jax.numpy as jnp
from jax import lax
from jax.experimental import pallas as pl
from jax.experimental.pallas import tpu as pltpu

F32 = jnp.float32
BF16 = jnp.bfloat16

D_MODEL = 1024
N_META = 16
CHUNK = 64
N_HEADS = 8
HEAD_DIM = 64
V_HEAD_DIM = 2 * HEAD_DIM
CONV_K = 31
D_FF = 2816
FFN_CONV_K = 3
NEG_INF = -1e30
LAMBDA_INIT = 0.8 - 0.6 * math.exp(0.0)
LOG2_E = math.log2(math.e)

SUBLANES = 8

FRONT = 512
SKIP = 256
HALO = 32
FFN_HALO = 8
CONV_ROWS = 32
DENOM_ROWS = 16
ATTN_W = 12
ATTN_TILE_BUFS = 8

TM_IN = 512
TM_ROW = 256
TQ = 256
TK = 512
VMEM_LIMIT = 56 * 1024 * 1024


def _dot(a, b):
    return jnp.dot(a, b, preferred_element_type=F32)


def _sigmoid(x):
    return 1.0 / (1.0 + jnp.exp(-x))


def _rms_rows(x, w, eps=1e-6):
    return x * lax.rsqrt(jnp.mean(x * x, axis=-1, keepdims=True) + eps) * w


def _params(n_axes):
    return pltpu.CompilerParams(
        dimension_semantics=("arbitrary",) * n_axes, vmem_limit_bytes=VMEM_LIMIT)


def _const_spec(shape):
    return pl.BlockSpec(shape, lambda *_: (0,) * len(shape))


def _front_or_frames(x_ref, meta_ref, is_front):
    tm, d = x_ref.shape
    front = jnp.concatenate([jnp.zeros((tm - N_META, d), F32), meta_ref[...]], axis=0)
    return jnp.where(is_front, front, x_ref[...])


def _in_proj_kernel(x_ref, meta_ref, nw_ref, w_ref, qt_ref, k_ref, vt_ref, u_ref):
    d = D_MODEL
    h = _front_or_frames(x_ref, meta_ref, pl.program_id(1) == 0)
    n = _rms_rows(h, nw_ref[...]).astype(BF16)
    q = _dot(n, w_ref[:, 0:d]) * (HEAD_DIM ** -0.5 * LOG2_E)
    for t in range(qt_ref.shape[0]):
        qt_ref[t] = q[t * TQ:(t + 1) * TQ, :].T.astype(BF16)
    k_ref[...] = _dot(n, w_ref[:, d:2 * d]).astype(BF16)
    v = _dot(n, w_ref[:, 2 * d:3 * d])
    for t in range(vt_ref.shape[0]):
        vt_ref[t] = v[t * TK:(t + 1) * TK, :].T.astype(BF16)
    a = _dot(n, w_ref[:, 3 * d:4 * d])
    g = _dot(n, w_ref[:, 4 * d:5 * d])
    u_ref[...] = a * _sigmoid(g)


def _in_proj(x, meta, norm_w, w_in):
    b, s, d = x.shape
    l = s + FRONT
    tm = TM_IN
    row = pl.BlockSpec((None, tm, d), lambda bi, i: (bi, i, 0))
    frames = pl.BlockSpec((None, tm, d), lambda bi, i: (bi, jnp.maximum(i - 1, 0), 0))
    qt = pl.BlockSpec((None, tm // TQ, d, TQ), lambda bi, i: (bi, i, 0, 0))
    vt = pl.BlockSpec((None, tm // TK, d, TK), lambda bi, i: (bi, i, 0, 0))
    return pl.pallas_call(
        _in_proj_kernel,
        grid=(b, l // tm),
        in_specs=[frames, _const_spec(meta.shape), _const_spec((1, d)), _const_spec((d, 5 * d))],
        out_specs=[qt, row, vt, row],
        out_shape=[jax.ShapeDtypeStruct((b, l // TQ, d, TQ), BF16),
                   jax.ShapeDtypeStruct((b, l, d), BF16),
                   jax.ShapeDtypeStruct((b, l // TK, d, TK), BF16),
                   jax.ShapeDtypeStruct((b, l, d), F32)],
        compiler_params=_params(2),
        name="in_proj",
    )(x, meta, norm_w, w_in)


_PLAIN, _DIAG, _FRONT_KEYS, _DIAG_AND_FRONT = 0, 1, 2, 3


def _attn_steps(n_qt):
    return sum(i // 2 + i % 2 for i in range(1, n_qt))


def _attn_kernel(qt_ref, k_ref, vt_ref, lq1_ref, lk1_ref, lq2_ref, lk2_ref, o_ref,
                 tbl_ref, onehot_ref, m_ref, acc_ref, *slot_refs, n_qt):
    tq, tk, w = TQ, TK, ATTN_W
    n_steps = _attn_steps(n_qt)
    n_slots = 2 * w
    slots = tuple(zip(slot_refs[0:n_slots], slot_refs[n_slots:2 * n_slots], slot_refs[2 * n_slots:]))

    key = lax.broadcasted_iota(jnp.int32, (tk, V_HEAD_DIM), 0)
    lane = lax.broadcasted_iota(jnp.int32, (tk, V_HEAD_DIM), 1)
    chunk_lane = lane < tk // CHUNK

    def bias(masked):
        return jnp.where(chunk_lane & masked, NEG_INF, 0.0).astype(BF16)

    tbl_ref[_PLAIN] = jnp.zeros((tk, V_HEAD_DIM), BF16)
    tbl_ref[_DIAG] = bias(key // CHUNK > lane)
    tbl_ref[_FRONT_KEYS] = bias(key < tk - N_META)
    tbl_ref[_DIAG_AND_FRONT] = bias(
        ((key < tk // 2) & (key // CHUNK > lane)) | ((key >= tk // 2) & (key < tk - N_META)))

    dim = lax.broadcasted_iota(jnp.int32, (V_HEAD_DIM, tq), 0)
    qry = lax.broadcasted_iota(jnp.int32, (V_HEAD_DIM, tq), 1)
    for part in range(tk // tq):
        chunk = part * (tq // CHUNK) + qry // CHUNK
        onehot_ref[part] = jnp.where(dim == chunk, 1.0, 0.0).astype(BF16)

    o_ref[0] = jnp.zeros((V_HEAD_DIM, tq), BF16)
    acc_ref[...] = jnp.zeros(acc_ref.shape, F32)
    m_ref[...] = jnp.full(m_ref.shape, NEG_INF, F32)
    ones_rows = jnp.ones((DENOM_ROWS, tk), BF16)

    def stacked_queries(i):
        qt = qt_ref[jnp.minimum(i, n_qt - 1)]
        zeros = jnp.zeros((HEAD_DIM, tq), BF16)
        onehot = onehot_ref[i & (tk // tq - 1)]
        return jnp.concatenate([
            jnp.concatenate([qt[0:HEAD_DIM], zeros], axis=1),
            jnp.concatenate([zeros, qt[HEAD_DIM:]], axis=1),
            jnp.concatenate([onehot, onehot], axis=1)], axis=0)

    half = tk // 2

    def merged(i, t):
        return jnp.logical_and(i & 1 == 0, t == i >> 1)

    def score_stage(i, t, slot):
        s_ref, mx_ref, a_ref = slots[slot]
        diag_front = merged(i, t)
        lo = pl.multiple_of(t * tk, tk)
        hi = pl.multiple_of(jnp.where(diag_front, half, t * tk + half), half)
        kt = jnp.concatenate([k_ref[pl.ds(lo, half), :], k_ref[pl.ds(hi, half), :]], axis=0)
        kind = jnp.where(diag_front, _DIAG_AND_FRONT,
                         jnp.where(t == 0, _FRONT_KEYS, jnp.where(t == i >> 1, _DIAG, _PLAIN)))
        s = _dot(jnp.concatenate([kt, tbl_ref[kind]], axis=1), stacked_queries(i))
        s_ref[...] = s
        first = t == (i & 1) ^ 1
        m_prev = jnp.where(first, NEG_INF, m_ref[...])
        m_new = jnp.maximum(m_prev, jnp.max(s, axis=0, keepdims=True))
        mx_ref[...] = m_new
        a_ref[...] = jnp.exp2(m_prev - m_new)
        m_ref[...] = m_new

    def softmax_value_stage(i, t, slot):
        s_ref, mx_ref, a_ref = slots[slot]
        p = jnp.exp2(s_ref[...] - mx_ref[...]).astype(BF16)
        t_hi = jnp.where(merged(i, t), 0, t)
        v_aug = jnp.concatenate(
            [jnp.concatenate([vt_ref[t][:, 0:half], vt_ref[t_hi][:, half:]], axis=1), ones_rows],
            axis=0)
        par = i & (ATTN_TILE_BUFS - 1)
        acc_ref[par] = a_ref[...] * acc_ref[par] + _dot(v_aug, p)

    lam = (jnp.exp(jnp.sum(lq1_ref[...] * lk1_ref[...], axis=-1, keepdims=True))
           - jnp.exp(jnp.sum(lq2_ref[...] * lk2_ref[...], axis=-1, keepdims=True))
           + LAMBDA_INIT)

    def finalize(i):
        par = i & (ATTN_TILE_BUFS - 1)
        l = acc_ref[par, V_HEAD_DIM:V_HEAD_DIM + 1, :]
        acc = acc_ref[par, 0:V_HEAD_DIM, :]
        o = acc[:, 0:tq] / l[:, 0:tq] - lam * (acc[:, tq:] / l[:, tq:])
        o_ref[i] = (o * lax.rsqrt(jnp.mean(o * o, axis=0, keepdims=True) + 1e-6)).astype(BF16)

    def advance(i, t):
        wrap = t == i >> 1
        return jnp.where(wrap, i + 1, i), jnp.where(wrap, i & 1, t + 1)

    def finalize_if_last(i, t):
        @pl.when(t == i >> 1)
        def _():
            finalize(i)

    def following(step, count):
        steps = [step]
        for _ in range(count - 1):
            steps.append(advance(*steps[-1]))
        return steps[:count]

    def block(first, base):
        steps = following(first, 2 * w)
        for j in range(w):
            score_stage(*steps[w + j], (base + w + j) % n_slots)
            softmax_value_stage(*steps[j], base + j)
        for step in steps[:w]:
            finalize_if_last(*step)
        return steps[w]

    start = (jnp.int32(1), jnp.int32(0))
    for j, step in enumerate(following(start, w)):
        score_stage(*step, j)
    n_trips = n_steps // n_slots
    first = lax.fori_loop(0, n_trips, lambda _, c: block(block(c, 0), w), start)
    for j, step in enumerate(following(first, n_steps - n_trips * n_slots)):
        if j >= w:
            score_stage(*step, j)
        softmax_value_stage(*step, j)
        finalize_if_last(*step)


def _attention(qt, k, vt, lq1, lk1, lq2, lk2):
    b, l, d = k.shape
    n_qt = l // TQ
    col = pl.BlockSpec((None, l, V_HEAD_DIM), lambda bi, h: (bi, 0, h))
    qtspec = pl.BlockSpec((None, n_qt, V_HEAD_DIM, TQ), lambda bi, h: (bi, 0, h, 0))
    vtspec = pl.BlockSpec((None, l // TK, V_HEAD_DIM, TK), lambda bi, h: (bi, 0, h, 0))
    lspec = _const_spec((1, HEAD_DIM))
    nq = 2 * TQ
    per_slot = lambda shape, dtype: [pltpu.VMEM(shape, dtype)] * (2 * ATTN_W)
    return pl.pallas_call(
        functools.partial(_attn_kernel, n_qt=n_qt),
        grid=(b, N_HEADS),
        in_specs=[qtspec, col, vtspec, lspec, lspec, lspec, lspec],
        out_specs=pl.BlockSpec((None, None, n_qt, V_HEAD_DIM, TQ), lambda bi, h: (bi, h, 0, 0, 0)),
        out_shape=jax.ShapeDtypeStruct((b, N_HEADS, n_qt, V_HEAD_DIM, TQ), BF16),
        scratch_shapes=[
            pltpu.VMEM((4, TK, V_HEAD_DIM), BF16),
            pltpu.VMEM((TK // TQ, V_HEAD_DIM, TQ), BF16),
            pltpu.VMEM((1, nq), F32),
            pltpu.VMEM((ATTN_TILE_BUFS, V_HEAD_DIM + DENOM_ROWS, nq), F32),
            *per_slot((TK, nq), F32),
            *per_slot((1, nq), F32),
            *per_slot((1, nq), F32),
        ],
        compiler_params=_params(2),
        name="diff_attn",
    )(qt, k, vt, lq1, lk1, lq2, lk2)


def _mix_kernel(u_ref, uh_ref, x_ref, meta_ref, yat_ref, sw_ref, cw_ref, cb_ref, lnw_ref,
                lnb_ref, wco_ref, nw_ref, wga_ref, wgc_ref, wo_ref, o_ref, ubuf_ref, act_ref):
    tm, d = act_ref.shape

    ubuf_ref[0:HALO, :] = uh_ref[...]
    ubuf_ref[HALO:, :] = u_ref[...]

    h = _front_or_frames(x_ref, meta_ref, pl.program_id(1) == 0)
    n = _rms_rows(h, nw_ref[...]).astype(BF16)
    y_att = jnp.concatenate([yat_ref[hd].astype(F32).T for hd in range(N_HEADS)], axis=1)
    gate_att = _sigmoid(_dot(n, wga_ref[...])) * (y_att * sw_ref[...])
    gate_conv = _sigmoid(_dot(n, wgc_ref[...]))

    for base in range(0, tm, CONV_ROWS):
        win = ubuf_ref[base:base + CONV_ROWS + HALO, :]
        groups = (CONV_ROWS // SUBLANES, SUBLANES, d)
        acc = jnp.broadcast_to(cb_ref[...][None], groups)
        for c in range(SUBLANES):
            taps = [t for t in range(CONV_K) if (HALO - CONV_K + 1 + t) % SUBLANES == c]
            span = max(HALO - CONV_K + 1 + t for t in taps) - c + CONV_ROWS
            shifted = win[c:c + span, :]
            for t in taps:
                q = HALO - CONV_K + 1 + t - c
                acc = acc + cw_ref[t][None] * shifted[q:q + CONV_ROWS, :].reshape(groups)
        acc = acc.reshape(CONV_ROWS, d)
        mu = jnp.mean(acc, axis=-1, keepdims=True)
        xc = acc - mu
        y = xc * lax.rsqrt(jnp.mean(xc * xc, axis=-1, keepdims=True) + 1e-5)
        y = y * lnw_ref[...] + lnb_ref[...]
        act_ref[base:base + CONV_ROWS, :] = (y * _sigmoid(y)).astype(BF16)

    m = gate_att + gate_conv * _dot(act_ref[...], wco_ref[...])
    o_ref[...] = h + _dot(m.astype(BF16), wo_ref[...])


def _mix(u, x, meta, y_att_t, subln_w, conv_w, conv_b, ln_w, ln_b, w_conv_out, norm_w, w_in,
         w_out):
    b, l, d = u.shape
    tm = TM_ROW
    skip = SKIP // tm
    row = pl.BlockSpec((None, tm, d), lambda bi, i: (bi, i + skip, 0))
    frames = pl.BlockSpec((None, tm, d), lambda bi, i: (bi, jnp.maximum(i - 1, 0), 0))
    halo = pl.BlockSpec((None, HALO, d), lambda bi, i: (bi, (i + skip) * (tm // HALO) - 1, 0))
    vec = _const_spec((1, d))
    assert tm == TQ
    heads = pl.BlockSpec((None, N_HEADS, None, V_HEAD_DIM, TQ), lambda bi, i: (bi, 0, i + skip, 0, 0))
    return pl.pallas_call(
        _mix_kernel,
        grid=(b, l // tm - skip),
        in_specs=[row, halo, frames, _const_spec(meta.shape), heads, vec, _const_spec(conv_w.shape),
                  _const_spec(conv_b.shape), vec, vec,
                  _const_spec(w_conv_out.shape), vec,
                  pl.BlockSpec((d, d), lambda *_: (0, 5)), pl.BlockSpec((d, d), lambda *_: (0, 6)),
                  _const_spec(w_out.shape)],
        out_specs=row,
        out_shape=jax.ShapeDtypeStruct((b, l, d), F32),
        scratch_shapes=[pltpu.VMEM((tm + HALO, d), F32), pltpu.VMEM((tm, d), BF16)],
        compiler_params=_params(2),
        name="conv_mix",
    )(u, u, x, meta, y_att_t, subln_w, conv_w, conv_b, ln_w, ln_b, w_conv_out, norm_w, w_in,
      w_in, w_out)


def _ffn_kernel(h_ref, nw_ref, wup_ref, fw_ref, fb_ref, wdn_ref, fnw_ref, o_ref,
                up_ref, act_ref):
    i = pl.program_id(1)
    tm = h_ref.shape[0]
    ff = D_FF
    lc = 256

    @pl.when(i == 0)
    def _():
        up_ref[0:FFN_HALO, :] = jnp.zeros((FFN_HALO, 2 * ff), F32)

    h = h_ref[...]
    n = _rms_rows(h, nw_ref[...]).astype(BF16)
    up_ref[FFN_HALO:, :] = _dot(n, wup_ref[...])

    def conv(c0):
        acc = jnp.broadcast_to(fb_ref[:, c0:c0 + lc], (tm, lc))
        for t in range(FFN_CONV_K):
            r0 = FFN_HALO - (FFN_CONV_K - 1) + t
            acc = acc + fw_ref[t:t + 1, c0:c0 + lc] * up_ref[r0:r0 + tm, c0:c0 + lc]
        return acc

    for c in range(ff // lc):
        gate = conv(c * lc)
        val = conv(ff + c * lc)
        act_ref[:, c * lc:(c + 1) * lc] = (gate * _sigmoid(gate) * val).astype(BF16)

    up_ref[0:FFN_HALO, :] = up_ref[tm:tm + FFN_HALO, :]
    h3 = h + _dot(act_ref[...], wdn_ref[...])
    o_ref[...] = _rms_rows(h3, fnw_ref[...])


def _ffn(h2, norm_w, w_up, ffn_w, ffn_b, w_down, final_w, seq):
    b, l, d = h2.shape
    tm = TM_ROW
    skip = SKIP // tm
    front_tiles = (FRONT - SKIP) // tm
    vec = _const_spec((1, d))
    return pl.pallas_call(
        _ffn_kernel,
        grid=(b, l // tm - skip),
        in_specs=[pl.BlockSpec((None, tm, d), lambda bi, i: (bi, i + skip, 0)), vec,
                  _const_spec(w_up.shape), _const_spec(ffn_w.shape),
                  _const_spec(ffn_b.shape), _const_spec(w_down.shape), vec],
        out_specs=pl.BlockSpec((None, tm, d),
                               lambda bi, i: (bi, jnp.maximum(i - front_tiles, 0), 0)),
        out_shape=jax.ShapeDtypeStruct((b, seq, d), F32),
        scratch_shapes=[pltpu.VMEM((tm + FFN_HALO, 2 * D_FF), F32),
                        pltpu.VMEM((tm, D_FF), BF16)],
        compiler_params=_params(2),
        name="ffn",
    )(h2, norm_w, w_up, ffn_w, ffn_b, w_down, final_w)


def kernel(x, meta_tokens, norm_mix_w, w_in, lambda_q1, lambda_k1, lambda_q2, lambda_k2,
           subln_w, conv_dw_w, conv_dw_b, conv_ln_w, conv_ln_b, w_conv_out, w_out,
           norm_ffn_w, w_up, ffn_dw_w, ffn_dw_b, w_down, norm_final_w):
    b, s, d = x.shape
    assert d == D_MODEL and w_in.shape[0] == 1, "single-layer trunk"
    assert s % TK == 0 and FRONT == TK == 2 * TQ and SKIP == TM_ROW and FRONT - SKIP == TM_ROW
    assert SKIP + HALO <= FRONT - N_META and TK // CHUNK <= V_HEAD_DIM and FRONT == TM_IN
    meta = meta_tokens.astype(F32)

    w_in_bf = w_in[0].astype(BF16)
    vec = lambda a: a.reshape(1, -1).astype(F32)

    qt, k, vt, u = _in_proj(x, meta, vec(norm_mix_w[0]), w_in_bf)
    y_att_t = _attention(qt, k, vt, vec(lambda_q1[0]), vec(lambda_k1[0]), vec(lambda_q2[0]),
                         vec(lambda_k2[0]))
    tile_rows = lambda a: jnp.broadcast_to(a.astype(F32)[..., None, :], a.shape[:-1] + (SUBLANES, d))
    subln_gain = vec(jnp.tile(subln_w[0], N_HEADS)) * (1.0 - LAMBDA_INIT)
    h2 = _mix(u, x, meta, y_att_t, subln_gain, tile_rows(conv_dw_w[0]),
              tile_rows(conv_dw_b[0]), vec(conv_ln_w[0]), vec(conv_ln_b[0]),
              w_conv_out[0].astype(BF16), vec(norm_mix_w[0]), w_in_bf, w_out[0].astype(BF16))
    return _ffn(h2, vec(norm_ffn_w[0]), w_up[0].astype(BF16), ffn_dw_w[0].astype(F32),
                vec(ffn_dw_b[0]), w_down[0].astype(BF16), vec(norm_final_w), s)
```

```python
import functools
import math

import jax
import jax.numpy as jnp
from jax import lax
from jax.experimental import pallas as pl
from jax.experimental.pallas import tpu as pltpu

F32 = jnp.float32
BF16 = jnp.bfloat16

D_MODEL = 1024
N_META = 16
CHUNK = 64
N_HEADS = 8
HEAD_DIM = 64
V_HEAD_DIM = 2 * HEAD_DIM
CONV_K = 31
D_FF = 2816
FFN_CONV_K = 3
NEG_INF = -1e30
LAMBDA_INIT = 0.8 - 0.6 * math.exp(0.0)
LOG2_E = math.log2(math.e)

SUBLANES = 8

FRONT = 512
SKIP = 256
HALO = 32
FFN_HALO = 8
CONV_ROWS = 32
DENOM_ROWS = 16
ATTN_W = 12
ATTN_TILE_BUFS = 8

TM_IN = 512
TM_ROW = 256
TQ = 256
TK = 512
VMEM_LIMIT = 56 * 1024 * 1024


def _dot(a, b):
    return jnp.dot(a, b, preferred_element_type=F32)


def _sigmoid(x):
    return 1.0 / (1.0 + jnp.exp(-x))


def _rms_rows(x, w, eps=1e-6):
    return x * lax.rsqrt(jnp.mean(x * x, axis=-1, keepdims=True) + eps) * w


def _params(n_axes):
    return pltpu.CompilerParams(
        dimension_semantics=("arbitrary",) * n_axes, vmem_limit_bytes=VMEM_LIMIT)


def _const_spec(shape):
    return pl.BlockSpec(shape, lambda *_: (0,) * len(shape))


def _front_or_frames(x_ref, meta_ref, is_front):
    tm, d = x_ref.shape
    front = jnp.concatenate([jnp.zeros((tm - N_META, d), F32), meta_ref[...]], axis=0)
    return jnp.where(is_front, front, x_ref[...])


def _in_proj_kernel(x_ref, meta_ref, nw_ref, w_ref, qt_ref, k_ref, vt_ref, u_ref):
    d = D_MODEL
    h = _front_or_frames(x_ref, meta_ref, pl.program_id(1) == 0)
    n = _rms_rows(h, nw_ref[...]).astype(BF16)
    q = _dot(n, w_ref[:, 0:d]) * (HEAD_DIM ** -0.5 * LOG2_E)
    for t in range(qt_ref.shape[0]):
        qt_ref[t] = q[t * TQ:(t + 1) * TQ, :].T.astype(BF16)
    k_ref[...] = _dot(n, w_ref[:, d:2 * d]).astype(BF16)
    v = _dot(n, w_ref[:, 2 * d:3 * d])
    for t in range(vt_ref.shape[0]):
        vt_ref[t] = v[t * TK:(t + 1) * TK, :].T.astype(BF16)
    a = _dot(n, w_ref[:, 3 * d:4 * d])
    g = _dot(n, w_ref[:, 4 * d:5 * d])
    u_ref[...] = a * _sigmoid(g)


def _in_proj(x, meta, norm_w, w_in):
    b, s, d = x.shape
    l = s + FRONT
    tm = TM_IN
    row = pl.BlockSpec((None, tm, d), lambda bi, i: (bi, i, 0))
    frames = pl.BlockSpec((None, tm, d), lambda bi, i: (bi, jnp.maximum(i - 1, 0), 0))
    qt = pl.BlockSpec((None, tm // TQ, d, TQ), lambda bi, i: (bi, i, 0, 0))
    vt = pl.BlockSpec((None, tm // TK, d, TK), lambda bi, i: (bi, i, 0, 0))
    return pl.pallas_call(
        _in_proj_kernel,
        grid=(b, l // tm),
        in_specs=[frames, _const_spec(meta.shape), _const_spec((1, d)), _const_spec((d, 5 * d))],
        out_specs=[qt, row, vt, row],
        out_shape=[jax.ShapeDtypeStruct((b, l // TQ, d, TQ), BF16),
                   jax.ShapeDtypeStruct((b, l, d), BF16),
                   jax.ShapeDtypeStruct((b, l // TK, d, TK), BF16),
                   jax.ShapeDtypeStruct((b, l, d), F32)],
        compiler_params=_params(2),
        name="in_proj",
    )(x, meta, norm_w, w_in)


_PLAIN, _DIAG, _FRONT_KEYS, _DIAG_AND_FRONT = 0, 1, 2, 3


def _attn_steps(n_qt):
    return sum(i // 2 + i % 2 for i in range(1, n_qt))


def _attn_kernel(qt_ref, k_ref, vt_ref, lq1_ref, lk1_ref, lq2_ref, lk2_ref, o_ref,
                 tbl_ref, onehot_ref, m_ref, acc_ref, *slot_refs, n_qt):
    tq, tk, w = TQ, TK, ATTN_W
    n_steps = _attn_steps(n_qt)
    n_slots = 2 * w
    slots = tuple(zip(slot_refs[0:n_slots], slot_refs[n_slots:2 * n_slots], slot_refs[2 * n_slots:]))

    key = lax.broadcasted_iota(jnp.int32, (tk, V_HEAD_DIM), 0)
    lane = lax.broadcasted_iota(jnp.int32, (tk, V_HEAD_DIM), 1)
    chunk_lane = lane < tk // CHUNK

    def bias(masked):
        return jnp.where(chunk_lane & masked, NEG_INF, 0.0).astype(BF16)

    tbl_ref[_PLAIN] = jnp.zeros((tk, V_HEAD_DIM), BF16)
    tbl_ref[_DIAG] = bias(key // CHUNK > lane)
    tbl_ref[_FRONT_KEYS] = bias(key < tk - N_META)
    tbl_ref[_DIAG_AND_FRONT] = bias(
        ((key < tk // 2) & (key // CHUNK > lane)) | ((key >= tk // 2) & (key < tk - N_META)))

    dim = lax.broadcasted_iota(jnp.int32, (V_HEAD_DIM, tq), 0)
    qry = lax.broadcasted_iota(jnp.int32, (V_HEAD_DIM, tq), 1)
    for part in range(tk // tq):
        chunk = part * (tq // CHUNK) + qry // CHUNK
        onehot_ref[part] = jnp.where(dim == chunk, 1.0, 0.0).astype(BF16)

    o_ref[0] = jnp.zeros((V_HEAD_DIM, tq), BF16)
    acc_ref[...] = jnp.zeros(acc_ref.shape, F32)
    m_ref[...] = jnp.full(m_ref.shape, NEG_INF, F32)
    ones_rows = jnp.ones((DENOM_ROWS, tk), BF16)

    def stacked_queries(i):
        qt = qt_ref[jnp.minimum(i, n_qt - 1)]
        zeros = jnp.zeros((HEAD_DIM, tq), BF16)
        onehot = onehot_ref[i & (tk // tq - 1)]
        return jnp.concatenate([
            jnp.concatenate([qt[0:HEAD_DIM], zeros], axis=1),
            jnp.concatenate([zeros, qt[HEAD_DIM:]], axis=1),
            jnp.concatenate([onehot, onehot], axis=1)], axis=0)

    half = tk // 2

    def merged(i, t):
        return jnp.logical_and(i & 1 == 0, t == i >> 1)

    def score_stage(i, t, slot):
        s_ref, mx_ref, a_ref = slots[slot]
        diag_front = merged(i, t)
        lo = pl.multiple_of(t * tk, tk)
        hi = pl.multiple_of(jnp.where(diag_front, half, t * tk + half), half)
        kt = jnp.concatenate([k_ref[pl.ds(lo, half), :], k_ref[pl.ds(hi, half), :]], axis=0)
        kind = jnp.where(diag_front, _DIAG_AND_FRONT,
                         jnp.where(t == 0, _FRONT_KEYS, jnp.where(t == i >> 1, _DIAG, _PLAIN)))
        s = _dot(jnp.concatenate([kt, tbl_ref[kind]], axis=1), stacked_queries(i))
        s_ref[...] = s
        first = t == (i & 1) ^ 1
        m_prev = jnp.where(first, NEG_INF, m_ref[...])
        m_new = jnp.maximum(m_prev, jnp.max(s, axis=0, keepdims=True))
        mx_ref[...] = m_new
        a_ref[...] = jnp.exp2(m_prev - m_new)
        m_ref[...] = m_new

    def softmax_value_stage(i, t, slot):
        s_ref, mx_ref, a_ref = slots[slot]
        p = jnp.exp2(s_ref[...] - mx_ref[...]).astype(BF16)
        t_hi = jnp.where(merged(i, t), 0, t)
        v_aug = jnp.concatenate(
            [jnp.concatenate([vt_ref[t][:, 0:half], vt_ref[t_hi][:, half:]], axis=1), ones_rows],
            axis=0)
        par = i & (ATTN_TILE_BUFS - 1)
        acc_ref[par] = a_ref[...] * acc_ref[par] + _dot(v_aug, p)

    lam = (jnp.exp(jnp.sum(lq1_ref[...] * lk1_ref[...], axis=-1, keepdims=True))
           - jnp.exp(jnp.sum(lq2_ref[...] * lk2_ref[...], axis=-1, keepdims=True))
           + LAMBDA_INIT)

    def finalize(i):
        par = i & (ATTN_TILE_BUFS - 1)
        l = acc_ref[par, V_HEAD_DIM:V_HEAD_DIM + 1, :]
        acc = acc_ref[par, 0:V_HEAD_DIM, :]
        o = acc[:, 0:tq] / l[:, 0:tq] - lam * (acc[:, tq:] / l[:, tq:])
        o_ref[i] = (o * lax.rsqrt(jnp.mean(o * o, axis=0, keepdims=True) + 1e-6)).astype(BF16)

    def advance(i, t):
        wrap = t == i >> 1
        return jnp.where(wrap, i + 1, i), jnp.where(wrap, i & 1, t + 1)

    def finalize_tiles(lo, hi):
        lax.fori_loop(lo, hi, lambda i, carry: (finalize(i), carry)[1], 0)

    def following(step, count):
        steps = [step]
        for _ in range(count - 1):
            steps.append(advance(*steps[-1]))
        return steps[:count]

    def block(first, base):
        steps = following(first, 2 * w)
        for j in range(w):
            score_stage(*steps[w + j], (base + w + j) % n_slots)
            softmax_value_stage(*steps[j], base + j)
        finalize_tiles(steps[0][0], steps[w][0])
        return steps[w]

    start = (jnp.int32(1), jnp.int32(0))
    for j, step in enumerate(following(start, w)):
        score_stage(*step, j)
    n_trips = n_steps // n_slots
    first = lax.fori_loop(0, n_trips, lambda _, c: block(block(c, 0), w), start)
    for j, step in enumerate(following(first, n_steps - n_trips * n_slots)):
        if j >= w:
            score_stage(*step, j)
        softmax_value_stage(*step, j)
    finalize_tiles(first[0], n_qt)


def _attention(qt, k, vt, lq1, lk1, lq2, lk2):
    b, l, d = k.shape
    n_qt = l // TQ
    col = pl.BlockSpec((None, l, V_HEAD_DIM), lambda bi, h: (bi, 0, h))
    qtspec = pl.BlockSpec((None, n_qt, V_HEAD_DIM, TQ), lambda bi, h: (bi, 0, h, 0))
    vtspec = pl.BlockSpec((None, l // TK, V_HEAD_DIM, TK), lambda bi, h: (bi, 0, h, 0))
    lspec = _const_spec((1, HEAD_DIM))
    nq = 2 * TQ
    per_slot = lambda shape, dtype: [pltpu.VMEM(shape, dtype)] * (2 * ATTN_W)
    return pl.pallas_call(
        functools.partial(_attn_kernel, n_qt=n_qt),
        grid=(b, N_HEADS),
        in_specs=[qtspec, col, vtspec, lspec, lspec, lspec, lspec],
        out_specs=pl.BlockSpec((None, None, n_qt, V_HEAD_DIM, TQ), lambda bi, h: (bi, h, 0, 0, 0)),
        out_shape=jax.ShapeDtypeStruct((b, N_HEADS, n_qt, V_HEAD_DIM, TQ), BF16),
        scratch_shapes=[
            pltpu.VMEM((4, TK, V_HEAD_DIM), BF16),
            pltpu.VMEM((TK // TQ, V_HEAD_DIM, TQ), BF16),
            pltpu.VMEM((1, nq), F32),
            pltpu.VMEM((ATTN_TILE_BUFS, V_HEAD_DIM + DENOM_ROWS, nq), F32),
            *per_slot((TK, nq), F32),
            *per_slot((1, nq), F32),
            *per_slot((1, nq), F32),
        ],
        compiler_params=_params(2),
        name="diff_attn",
    )(qt, k, vt, lq1, lk1, lq2, lk2)


def _mix_kernel(u_ref, uh_ref, x_ref, meta_ref, yat_ref, sw_ref, cw_ref, cb_ref, lnw_ref,
                lnb_ref, wco_ref, nw_ref, wga_ref, wgc_ref, wo_ref, o_ref, ubuf_ref, act_ref):
    tm, d = act_ref.shape

    ubuf_ref[0:HALO, :] = uh_ref[...]
    ubuf_ref[HALO:, :] = u_ref[...]

    h = _front_or_frames(x_ref, meta_ref, pl.program_id(1) == 0)
    n = _rms_rows(h, nw_ref[...]).astype(BF16)
    y_att = jnp.concatenate([yat_ref[hd].astype(F32).T for hd in range(N_HEADS)], axis=1)
    gate_att = _sigmoid(_dot(n, wga_ref[...])) * (y_att * sw_ref[...])
    gate_conv = _sigmoid(_dot(n, wgc_ref[...]))

    for base in range(0, tm, CONV_ROWS):
        win = ubuf_ref[base:base + CONV_ROWS + HALO, :]
        groups = (CONV_ROWS // SUBLANES, SUBLANES, d)
        acc = jnp.broadcast_to(cb_ref[...][None], groups)
        for c in range(SUBLANES):
            taps = [t for t in range(CONV_K) if (HALO - CONV_K + 1 + t) % SUBLANES == c]
            span = max(HALO - CONV_K + 1 + t for t in taps) - c + CONV_ROWS
            shifted = win[c:c + span, :]
            for t in taps:
                q = HALO - CONV_K + 1 + t - c
                acc = acc + cw_ref[t][None] * shifted[q:q + CONV_ROWS, :].reshape(groups)
        acc = acc.reshape(CONV_ROWS, d)
        mu = jnp.mean(acc, axis=-1, keepdims=True)
        xc = acc - mu
        y = xc * lax.rsqrt(jnp.mean(xc * xc, axis=-1, keepdims=True) + 1e-5)
        y = y * lnw_ref[...] + lnb_ref[...]
        act_ref[base:base + CONV_ROWS, :] = (y * _sigmoid(y)).astype(BF16)

    m = gate_att + gate_conv * _dot(act_ref[...], wco_ref[...])
    o_ref[...] = h + _dot(m.astype(BF16), wo_ref[...])


def _mix(u, x, meta, y_att_t, subln_w, conv_w, conv_b, ln_w, ln_b, w_conv_out, norm_w, w_in,
         w_out):
    b, l, d = u.shape
    tm = TM_ROW
    skip = SKIP // tm
    row = pl.BlockSpec((None, tm, d), lambda bi, i: (bi, i + skip, 0))
    frames = pl.BlockSpec((None, tm, d), lambda bi, i: (bi, jnp.maximum(i - 1, 0), 0))
    halo = pl.BlockSpec((None, HALO, d), lambda bi, i: (bi, (i + skip) * (tm // HALO) - 1, 0))
    vec = _const_spec((1, d))
    assert tm == TQ
    heads = pl.BlockSpec((None, N_HEADS, None, V_HEAD_DIM, TQ), lambda bi, i: (bi, 0, i + skip, 0, 0))
    return pl.pallas_call(
        _mix_kernel,
        grid=(b, l // tm - skip),
        in_specs=[row, halo, frames, _const_spec(meta.shape), heads, vec, _const_spec(conv_w.shape),
                  _const_spec(conv_b.shape), vec, vec,
                  _const_spec(w_conv_out.shape), vec,
                  pl.BlockSpec((d, d), lambda *_: (0, 5)), pl.BlockSpec((d, d), lambda *_: (0, 6)),
                  _const_spec(w_out.shape)],
        out_specs=row,
        out_shape=jax.ShapeDtypeStruct((b, l, d), F32),
        scratch_shapes=[pltpu.VMEM((tm + HALO, d), F32), pltpu.VMEM((tm, d), BF16)],
        compiler_params=_params(2),
        name="conv_mix",
    )(u, u, x, meta, y_att_t, subln_w, conv_w, conv_b, ln_w, ln_b, w_conv_out, norm_w, w_in,
      w_in, w_out)


def _ffn_kernel(h_ref, nw_ref, wup_ref, fw_ref, fb_ref, wdn_ref, fnw_ref, o_ref,
                up_ref, act_ref):
    i = pl.program_id(1)
    tm = h_ref.shape[0]
    ff = D_FF
    lc = 256

    @pl.when(i == 0)
    def _():
        up_ref[0:FFN_HALO, :] = jnp.zeros((FFN_HALO, 2 * ff), F32)

    h = h_ref[...]
    n = _rms_rows(h, nw_ref[...]).astype(BF16)
    up_ref[FFN_HALO:, :] = _dot(n, wup_ref[...])

    def conv(c0):
        acc = jnp.broadcast_to(fb_ref[:, c0:c0 + lc], (tm, lc))
        for t in range(FFN_CONV_K):
            r0 = FFN_HALO - (FFN_CONV_K - 1) + t
            acc = acc + fw_ref[t:t + 1, c0:c0 + lc] * up_ref[r0:r0 + tm, c0:c0 + lc]
        return acc

    for c in range(ff // lc):
        gate = conv(c * lc)
        val = conv(ff + c * lc)
        act_ref[:, c * lc:(c + 1) * lc] = (gate * _sigmoid(gate) * val).astype(BF16)

    up_ref[0:FFN_HALO, :] = up_ref[tm:tm + FFN_HALO, :]
    h3 = h + _dot(act_ref[...], wdn_ref[...])
    o_ref[...] = _rms_rows(h3, fnw_ref[...])


def _ffn(h2, norm_w, w_up, ffn_w, ffn_b, w_down, final_w, seq):
    b, l, d = h2.shape
    tm = TM_ROW
    skip = SKIP // tm
    front_tiles = (FRONT - SKIP) // tm
    vec = _const_spec((1, d))
    return pl.pallas_call(
        _ffn_kernel,
        grid=(b, l // tm - skip),
        in_specs=[pl.BlockSpec((None, tm, d), lambda bi, i: (bi, i + skip, 0)), vec,
                  _const_spec(w_up.shape), _const_spec(ffn_w.shape),
                  _const_spec(ffn_b.shape), _const_spec(w_down.shape), vec],
        out_specs=pl.BlockSpec((None, tm, d),
                               lambda bi, i: (bi, jnp.maximum(i - front_tiles, 0), 0)),
        out_shape=jax.ShapeDtypeStruct((b, seq, d), F32),
        scratch_shapes=[pltpu.VMEM((tm + FFN_HALO, 2 * D_FF), F32),
                        pltpu.VMEM((tm, D_FF), BF16)],
        compiler_params=_params(2),
        name="ffn",
    )(h2, norm_w, w_up, ffn_w, ffn_b, w_down, final_w)


def kernel(x, meta_tokens, norm_mix_w, w_in, lambda_q1, lambda_k1, lambda_q2, lambda_k2,
           subln_w, conv_dw_w, conv_dw_b, conv_ln_w, conv_ln_b, w_conv_out, w_out,
           norm_ffn_w, w_up, ffn_dw_w, ffn_dw_b, w_down, norm_final_w):
    b, s, d = x.shape
    assert d == D_MODEL and w_in.shape[0] == 1, "single-layer trunk"
    assert s % TK == 0 and FRONT == TK == 2 * TQ and SKIP == TM_ROW and FRONT - SKIP == TM_ROW
    assert SKIP + HALO <= FRONT - N_META and TK // CHUNK <= V_HEAD_DIM and FRONT == TM_IN
    meta = meta_tokens.astype(F32)

    w_in_bf = w_in[0].astype(BF16)
    vec = lambda a: a.reshape(1, -1).astype(F32)

    qt, k, vt, u = _in_proj(x, meta, vec(norm_mix_w[0]), w_in_bf)
    y_att_t = _attention(qt, k, vt, vec(lambda_q1[0]), vec(lambda_k1[0]), vec(lambda_q2[0]),
                         vec(lambda_k2[0]))
    tile_rows = lambda a: jnp.broadcast_to(a.astype(F32)[..., None, :], a.shape[:-1] + (SUBLANES, d))
    subln_gain = vec(jnp.tile(subln_w[0], N_HEADS)) * (1.0 - LAMBDA_INIT)
    h2 = _mix(u, x, meta, y_att_t, subln_gain, tile_rows(conv_dw_w[0]),
              tile_rows(conv_dw_b[0]), vec(conv_ln_w[0]), vec(conv_ln_b[0]),
              w_conv_out[0].astype(BF16), vec(norm_mix_w[0]), w_in_bf, w_out[0].astype(BF16))
    return _ffn(h2, vec(norm_ffn_w[0]), w_up[0].astype(BF16), ffn_dw_w[0].astype(F32),
                vec(ffn_dw_b[0]), w_down[0].astype(BF16), vec(norm_final_w), s)
```

```python
import functools
import math

import jax
import jax.numpy as jnp
from jax import lax
from jax.experimental import pallas as pl
from jax.experimental.pallas import tpu as pltpu

F32 = jnp.float32
BF16 = jnp.bfloat16

D_MODEL = 1024
N_META = 16
CHUNK = 64
N_HEADS = 8
HEAD_DIM = 64
V_HEAD_DIM = 2 * HEAD_DIM
CONV_K = 31
D_FF = 2816
FFN_CONV_K = 3
NEG_INF = -1e30
LAMBDA_INIT = 0.8 - 0.6 * math.exp(0.0)
LOG2_E = math.log2(math.e)

SUBLANES = 8

FRONT = 512
SKIP = 256
HALO = 32
FFN_HALO = 8
CONV_ROWS = 32
DENOM_ROWS = 16
ATTN_W = 12
ATTN_TILE_BUFS = 8

TM_IN = 512
TM_ROW = 256
TQ = 256
TK = 512
VMEM_LIMIT = 56 * 1024 * 1024


def _dot(a, b):
    return jnp.dot(a, b, preferred_element_type=F32)


def _sigmoid(x):
    return 1.0 / (1.0 + jnp.exp(-x))


def _rms_rows(x, w, eps=1e-6):
    return x * lax.rsqrt(jnp.mean(x * x, axis=-1, keepdims=True) + eps) * w


def _params(n_axes):
    return pltpu.CompilerParams(
        dimension_semantics=("arbitrary",) * n_axes, vmem_limit_bytes=VMEM_LIMIT)


def _const_spec(shape):
    return pl.BlockSpec(shape, lambda *_: (0,) * len(shape))


def _front_or_frames(x_ref, meta_ref, is_front):
    tm, d = x_ref.shape
    front = jnp.concatenate([jnp.zeros((tm - N_META, d), F32), meta_ref[...]], axis=0)
    return jnp.where(is_front, front, x_ref[...])


def _in_proj_kernel(x_ref, meta_ref, nw_ref, w_ref, qt_ref, k_ref, vt_ref, u_ref):
    d = D_MODEL
    h = _front_or_frames(x_ref, meta_ref, pl.program_id(1) == 0)
    n = _rms_rows(h, nw_ref[...]).astype(BF16)
    q = _dot(n, w_ref[:, 0:d]) * (HEAD_DIM ** -0.5 * LOG2_E)
    for t in range(qt_ref.shape[0]):
        qt_ref[t] = q[t * TQ:(t + 1) * TQ, :].T.astype(BF16)
    k_ref[...] = _dot(n, w_ref[:, d:2 * d]).astype(BF16)
    v = _dot(n, w_ref[:, 2 * d:3 * d])
    for t in range(vt_ref.shape[0]):
        vt_ref[t] = v[t * TK:(t + 1) * TK, :].T.astype(BF16)
    a = _dot(n, w_ref[:, 3 * d:4 * d])
    g = _dot(n, w_ref[:, 4 * d:5 * d])
    u_ref[...] = a * _sigmoid(g)


def _in_proj(x, meta, norm_w, w_in):
    b, s, d = x.shape
    l = s + FRONT
    tm = TM_IN
    row = pl.BlockSpec((None, tm, d), lambda bi, i: (bi, i, 0))
    frames = pl.BlockSpec((None, tm, d), lambda bi, i: (bi, jnp.maximum(i - 1, 0), 0))
    qt = pl.BlockSpec((None, tm // TQ, d, TQ), lambda bi, i: (bi, i, 0, 0))
    vt = pl.BlockSpec((None, tm // TK, d, TK), lambda bi, i: (bi, i, 0, 0))
    return pl.pallas_call(
        _in_proj_kernel,
        grid=(b, l // tm),
        in_specs=[frames, _const_spec(meta.shape), _const_spec((1, d)), _const_spec((d, 5 * d))],
        out_specs=[qt, row, vt, row],
        out_shape=[jax.ShapeDtypeStruct((b, l // TQ, d, TQ), BF16),
                   jax.ShapeDtypeStruct((b, l, d), BF16),
                   jax.ShapeDtypeStruct((b, l // TK, d, TK), BF16),
                   jax.ShapeDtypeStruct((b, l, d), F32)],
        compiler_params=_params(2),
        name="in_proj",
    )(x, meta, norm_w, w_in)


_PLAIN, _DIAG, _FRONT_KEYS, _DIAG_AND_FRONT = 0, 1, 2, 3


def _attn_steps(n_qt):
    return sum(i // 2 + i % 2 for i in range(1, n_qt))


def _attn_kernel(qt_ref, k_ref, vt_ref, lq1_ref, lk1_ref, lq2_ref, lk2_ref, o_ref,
                 tbl_ref, onehot_ref, m_ref, acc_ref, *slot_refs, n_qt):
    tq, tk, w = TQ, TK, ATTN_W
    n_steps = _attn_steps(n_qt)
    n_slots = 2 * w
    slots = tuple(zip(slot_refs[0:n_slots], slot_refs[n_slots:2 * n_slots], slot_refs[2 * n_slots:]))

    key = lax.broadcasted_iota(jnp.int32, (tk, V_HEAD_DIM), 0)
    lane = lax.broadcasted_iota(jnp.int32, (tk, V_HEAD_DIM), 1)
    chunk_lane = lane < tk // CHUNK

    def bias(masked):
        return jnp.where(chunk_lane & masked, NEG_INF, 0.0).astype(BF16)

    tbl_ref[_PLAIN] = jnp.zeros((tk, V_HEAD_DIM), BF16)
    tbl_ref[_DIAG] = bias(key // CHUNK > lane)
    tbl_ref[_FRONT_KEYS] = bias(key < tk - N_META)
    tbl_ref[_DIAG_AND_FRONT] = bias(
        ((key < tk // 2) & (key // CHUNK > lane)) | ((key >= tk // 2) & (key < tk - N_META)))

    dim = lax.broadcasted_iota(jnp.int32, (V_HEAD_DIM, tq), 0)
    qry = lax.broadcasted_iota(jnp.int32, (V_HEAD_DIM, tq), 1)
    for part in range(tk // tq):
        chunk = part * (tq // CHUNK) + qry // CHUNK
        onehot_ref[part] = jnp.where(dim == chunk, 1.0, 0.0).astype(BF16)

    o_ref[0] = jnp.zeros((V_HEAD_DIM, tq), BF16)
    acc_ref[...] = jnp.zeros(acc_ref.shape, F32)
    m_ref[...] = jnp.full(m_ref.shape, NEG_INF, F32)
    ones_rows = jnp.ones((DENOM_ROWS, tk), BF16)

    def stacked_queries(i):
        qt = qt_ref[jnp.minimum(i, n_qt - 1)]
        zeros = jnp.zeros((HEAD_DIM, tq), BF16)
        onehot = onehot_ref[i & (tk // tq - 1)]
        return jnp.concatenate([
            jnp.concatenate([qt[0:HEAD_DIM], zeros], axis=1),
            jnp.concatenate([zeros, qt[HEAD_DIM:]], axis=1),
            jnp.concatenate([onehot, onehot], axis=1)], axis=0)

    half = tk // 2

    def merged(i, t):
        return jnp.logical_and(i & 1 == 0, t == i >> 1)

    def score_stage(i, t, slot):
        s_ref, mx_ref, a_ref = slots[slot]
        diag_front = merged(i, t)
        lo = pl.multiple_of(t * tk, tk)
        hi = pl.multiple_of(jnp.where(diag_front, half, t * tk + half), half)
        kt = jnp.concatenate([k_ref[pl.ds(lo, half), :], k_ref[pl.ds(hi, half), :]], axis=0)
        kind = jnp.where(diag_front, _DIAG_AND_FRONT,
                         jnp.where(t == 0, _FRONT_KEYS, jnp.where(t == i >> 1, _DIAG, _PLAIN)))
        s = _dot(jnp.concatenate([kt, tbl_ref[kind]], axis=1), stacked_queries(i))
        s_ref[...] = s
        first = t == (i & 1) ^ 1
        m_prev = jnp.where(first, NEG_INF, m_ref[...])
        m_new = jnp.maximum(m_prev, jnp.max(s, axis=0, keepdims=True))
        mx_ref[...] = m_new
        a_ref[...] = jnp.exp2(m_prev - m_new)
        m_ref[...] = m_new

    def softmax_value_stage(i, t, slot):
        s_ref, mx_ref, a_ref = slots[slot]
        p = jnp.exp2(s_ref[...] - mx_ref[...]).astype(BF16)
        t_hi = jnp.where(merged(i, t), 0, t)
        v_aug = jnp.concatenate(
            [jnp.concatenate([vt_ref[t][:, 0:half], vt_ref[t_hi][:, half:]], axis=1), ones_rows],
            axis=0)
        par = i & (ATTN_TILE_BUFS - 1)
        acc_ref[par] = a_ref[...] * acc_ref[par] + _dot(v_aug, p)

    lam = (jnp.exp(jnp.sum(lq1_ref[...] * lk1_ref[...], axis=-1, keepdims=True))
           - jnp.exp(jnp.sum(lq2_ref[...] * lk2_ref[...], axis=-1, keepdims=True))
           + LAMBDA_INIT)

    def finalize(i):
        par = i & (ATTN_TILE_BUFS - 1)
        l = acc_ref[par, V_HEAD_DIM:V_HEAD_DIM + 1, :]
        acc = acc_ref[par, 0:V_HEAD_DIM, :]
        o = acc[:, 0:tq] / l[:, 0:tq] - lam * (acc[:, tq:] / l[:, tq:])
        o_ref[i] = (o * lax.rsqrt(jnp.mean(o * o, axis=0, keepdims=True) + 1e-6)).astype(BF16)

    def advance(i, t):
        wrap = t == i >> 1
        return jnp.where(wrap, i + 1, i), jnp.where(wrap, i & 1, t + 1)

    def finalize_tiles(lo, hi):
        lax.fori_loop(lo, hi, lambda i, carry: (finalize(i), carry)[1], 0)

    def following(step, count):
        steps = [step]
        for _ in range(count - 1):
            steps.append(advance(*steps[-1]))
        return steps[:count]

    def block(first, base):
        steps = following(first, 2 * w)
        for j in range(w):
            score_stage(*steps[w + j], (base + w + j) % n_slots)
            softmax_value_stage(*steps[j], base + j)
        finalize_tiles(steps[0][0], steps[w][0])
        return steps[w]

    start = (jnp.int32(1), jnp.int32(0))
    for j, step in enumerate(following(start, w)):
        score_stage(*step, j)
    n_trips = n_steps // n_slots
    first = lax.fori_loop(0, n_trips, lambda _, c: block(block(c, 0), w), start)
    for j, step in enumerate(following(first, n_steps - n_trips * n_slots)):
        if j >= w:
            score_stage(*step, j)
        softmax_value_stage(*step, j)
    finalize_tiles(first[0], n_qt)


def _attention(qt, k, vt, lq1, lk1, lq2, lk2):
    b, l, d = k.shape
    n_qt = l // TQ
    col = pl.BlockSpec((None, l, V_HEAD_DIM), lambda bi, h: (bi, 0, h))
    qtspec = pl.BlockSpec((None, n_qt, V_HEAD_DIM, TQ), lambda bi, h: (bi, 0, h, 0))
    vtspec = pl.BlockSpec((None, l // TK, V_HEAD_DIM, TK), lambda bi, h: (bi, 0, h, 0))
    lspec = _const_spec((1, HEAD_DIM))
    nq = 2 * TQ
    per_slot = lambda shape, dtype: [pltpu.VMEM(shape, dtype)] * (2 * ATTN_W)
    return pl.pallas_call(
        functools.partial(_attn_kernel, n_qt=n_qt),
        grid=(b, N_HEADS),
        in_specs=[qtspec, col, vtspec, lspec, lspec, lspec, lspec],
        out_specs=pl.BlockSpec((None, None, n_qt, V_HEAD_DIM, TQ), lambda bi, h: (bi, h, 0, 0, 0)),
        out_shape=jax.ShapeDtypeStruct((b, N_HEADS, n_qt, V_HEAD_DIM, TQ), BF16),
        scratch_shapes=[
            pltpu.VMEM((4, TK, V_HEAD_DIM), BF16),
            pltpu.VMEM((TK // TQ, V_HEAD_DIM, TQ), BF16),
            pltpu.VMEM((1, nq), F32),
            pltpu.VMEM((ATTN_TILE_BUFS, V_HEAD_DIM + DENOM_ROWS, nq), F32),
            *per_slot((TK, nq), F32),
            *per_slot((1, nq), F32),
            *per_slot((1, nq), F32),
        ],
        compiler_params=_params(2),
        name="diff_attn",
    )(qt, k, vt, lq1, lk1, lq2, lk2)


def _mix_kernel(u_ref, uh_ref, x_ref, meta_ref, yat_ref, sw_ref, cw_ref, cb_ref, lnw_ref,
                lnb_ref, wco_ref, nw_ref, wga_ref, wgc_ref, wo_ref, o_ref, ubuf_ref, act_ref):
    tm, d = act_ref.shape

    ubuf_ref[0:HALO, :] = uh_ref[...]
    ubuf_ref[HALO:, :] = u_ref[...]

    h = _front_or_frames(x_ref, meta_ref, pl.program_id(1) == 0)
    n = _rms_rows(h, nw_ref[...]).astype(BF16)
    y_att = jnp.concatenate([yat_ref[hd].astype(F32).T for hd in range(N_HEADS)], axis=1)
    gate_att = _sigmoid(_dot(n, wga_ref[...])) * (y_att * sw_ref[...])
    gate_conv = _sigmoid(_dot(n, wgc_ref[...]))

    for base in range(0, tm, CONV_ROWS):
        win = ubuf_ref[base:base + CONV_ROWS + HALO, :]
        groups = (CONV_ROWS // SUBLANES, SUBLANES, d)
        acc = jnp.broadcast_to(cb_ref[...][None], groups)
        for c in range(SUBLANES):
            taps = [t for t in range(CONV_K) if (HALO - CONV_K + 1 + t) % SUBLANES == c]
            span = max(HALO - CONV_K + 1 + t for t in taps) - c + CONV_ROWS
            shifted = win[c:c + span, :]
            for t in taps:
                q = HALO - CONV_K + 1 + t - c
                acc = acc + cw_ref[t][None] * shifted[q:q + CONV_ROWS, :].reshape(groups)
        acc = acc.reshape(CONV_ROWS, d)
        mu = jnp.mean(acc, axis=-1, keepdims=True)
        xc = acc - mu
        y = xc * lax.rsqrt(jnp.mean(xc * xc, axis=-1, keepdims=True) + 1e-5)
        y = y * lnw_ref[...] + lnb_ref[...]
        act_ref[base:base + CONV_ROWS, :] = (y * _sigmoid(y)).astype(BF16)

    m = gate_att + gate_conv * _dot(act_ref[...], wco_ref[...])
    o_ref[...] = h + _dot(m.astype(BF16), wo_ref[...])


def _mix(u, x, meta, y_att_t, subln_w, conv_w, conv_b, ln_w, ln_b, w_conv_out, norm_w, w_in,
         w_out):
    b, l, d = u.shape
    tm = TM_ROW
    skip = SKIP // tm
    row = pl.BlockSpec((None, tm, d), lambda bi, i: (bi, i + skip, 0))
    frames = pl.BlockSpec((None, tm, d), lambda bi, i: (bi, jnp.maximum(i - 1, 0), 0))
    halo = pl.BlockSpec((None, HALO, d), lambda bi, i: (bi, (i + skip) * (tm // HALO) - 1, 0))
    vec = _const_spec((1, d))
    assert tm == TQ
    heads = pl.BlockSpec((None, N_HEADS, None, V_HEAD_DIM, TQ), lambda bi, i: (bi, 0, i + skip, 0, 0))
    return pl.pallas_call(
        _mix_kernel,
        grid=(b, l // tm - skip),
        in_specs=[row, halo, frames, _const_spec(meta.shape), heads, vec, _const_spec(conv_w.shape),
                  _const_spec(conv_b.shape), vec, vec,
                  _const_spec(w_conv_out.shape), vec,
                  pl.BlockSpec((d, d), lambda *_: (0, 5)), pl.BlockSpec((d, d), lambda *_: (0, 6)),
                  _const_spec(w_out.shape)],
        out_specs=pl.BlockSpec((None, tm, d), lambda bi, i: (bi, i, 0)),
        out_shape=jax.ShapeDtypeStruct((b, l - SKIP, d), F32),
        scratch_shapes=[pltpu.VMEM((tm + HALO, d), F32), pltpu.VMEM((tm, d), BF16)],
        compiler_params=_params(2),
        name="conv_mix",
    )(u, u, x, meta, y_att_t, subln_w, conv_w, conv_b, ln_w, ln_b, w_conv_out, norm_w, w_in,
      w_in, w_out)


def _ffn_kernel(h_ref, nw_ref, wup_ref, fw_ref, fb_ref, wdn_ref, fnw_ref, o_ref,
                up_ref, act_ref):
    i = pl.program_id(1)
    tm = h_ref.shape[0]
    ff = D_FF
    lc = 256

    @pl.when(i == 0)
    def _():
        up_ref[0:FFN_HALO, :] = jnp.zeros((FFN_HALO, 2 * ff), F32)

    h = h_ref[...]
    n = _rms_rows(h, nw_ref[...]).astype(BF16)
    up_ref[FFN_HALO:, :] = _dot(n, wup_ref[...])

    def conv(c0):
        acc = jnp.broadcast_to(fb_ref[:, c0:c0 + lc], (tm, lc))
        for t in range(FFN_CONV_K):
            r0 = FFN_HALO - (FFN_CONV_K - 1) + t
            acc = acc + fw_ref[t:t + 1, c0:c0 + lc] * up_ref[r0:r0 + tm, c0:c0 + lc]
        return acc

    for c in range(ff // lc):
        gate = conv(c * lc)
        val = conv(ff + c * lc)
        act_ref[:, c * lc:(c + 1) * lc] = (gate * _sigmoid(gate) * val).astype(BF16)

    up_ref[0:FFN_HALO, :] = up_ref[tm:tm + FFN_HALO, :]
    h3 = h + _dot(act_ref[...], wdn_ref[...])
    o_ref[...] = _rms_rows(h3, fnw_ref[...])


def _ffn(h2, norm_w, w_up, ffn_w, ffn_b, w_down, final_w, seq):
    b, rows, d = h2.shape
    tm = TM_ROW
    front_tiles = (FRONT - SKIP) // tm
    vec = _const_spec((1, d))
    return pl.pallas_call(
        _ffn_kernel,
        grid=(b, rows // tm),
        in_specs=[pl.BlockSpec((None, tm, d), lambda bi, i: (bi, i, 0)), vec,
                  _const_spec(w_up.shape), _const_spec(ffn_w.shape),
                  _const_spec(ffn_b.shape), _const_spec(w_down.shape), vec],
        out_specs=pl.BlockSpec((None, tm, d),
                               lambda bi, i: (bi, jnp.maximum(i - front_tiles, 0), 0)),
        out_shape=jax.ShapeDtypeStruct((b, seq, d), F32),
        scratch_shapes=[pltpu.VMEM((tm + FFN_HALO, 2 * D_FF), F32),
                        pltpu.VMEM((tm, D_FF), BF16)],
        compiler_params=_params(2),
        name="ffn",
    )(h2, norm_w, w_up, ffn_w, ffn_b, w_down, final_w)


def kernel(x, meta_tokens, norm_mix_w, w_in, lambda_q1, lambda_k1, lambda_q2, lambda_k2,
           subln_w, conv_dw_w, conv_dw_b, conv_ln_w, conv_ln_b, w_conv_out, w_out,
           norm_ffn_w, w_up, ffn_dw_w, ffn_dw_b, w_down, norm_final_w):
    b, s, d = x.shape
    assert d == D_MODEL and w_in.shape[0] == 1, "single-layer trunk"
    assert s % TK == 0 and FRONT == TK == 2 * TQ and SKIP == TM_ROW and FRONT - SKIP == TM_ROW
    assert SKIP + HALO <= FRONT - N_META and TK // CHUNK <= V_HEAD_DIM and FRONT == TM_IN
    meta = meta_tokens.astype(F32)

    w_in_bf = w_in[0].astype(BF16)
    vec = lambda a: a.reshape(1, -1).astype(F32)

    qt, k, vt, u = _in_proj(x, meta, vec(norm_mix_w[0]), w_in_bf)
    y_att_t = _attention(qt, k, vt, vec(lambda_q1[0]), vec(lambda_k1[0]), vec(lambda_q2[0]),
                         vec(lambda_k2[0]))
    tile_rows = lambda a: jnp.broadcast_to(a.astype(F32)[..., None, :], a.shape[:-1] + (SUBLANES, d))
    subln_gain = vec(jnp.tile(subln_w[0], N_HEADS)) * (1.0 - LAMBDA_INIT)
    h2 = _mix(u, x, meta, y_att_t, subln_gain, tile_rows(conv_dw_w[0]),
              tile_rows(conv_dw_b[0]), vec(conv_ln_w[0]), vec(conv_ln_b[0]),
              w_conv_out[0].astype(BF16), vec(norm_mix_w[0]), w_in_bf, w_out[0].astype(BF16))
    return _ffn(h2, vec(norm_ffn_w[0]), w_up[0].astype(BF16), ffn_dw_w[0].astype(F32),
                vec(ffn_dw_b[0]), w_down[0].astype(BF16), vec(norm_final_w), s)
```

```python
import functools
import math

import jax
import jax.numpy as jnp
from jax import lax
from jax.experimental import pallas as pl
from jax.experimental.pallas import tpu as pltpu

F32 = jnp.float32
BF16 = jnp.bfloat16

D_MODEL = 1024
N_META = 16
CHUNK = 64
N_HEADS = 8
HEAD_DIM = 64
V_HEAD_DIM = 2 * HEAD_DIM
CONV_K = 31
D_FF = 2816
FFN_CONV_K = 3
NEG_INF = -1e30
LAMBDA_INIT = 0.8 - 0.6 * math.exp(0.0)
LOG2_E = math.log2(math.e)

SUBLANES = 8

FRONT = 512
SKIP = 256
HALO = 32
FFN_HALO = 8
CONV_ROWS = 32
DENOM_ROWS = 16
ATTN_W = 12
ATTN_TILE_BUFS = 8

TM_IN = 512
TM_ROW = 256
TQ = 256
TK = 512
VMEM_LIMIT = 56 * 1024 * 1024


def _dot(a, b):
    return jnp.dot(a, b, preferred_element_type=F32)


def _sigmoid(x):
    return 1.0 / (1.0 + jnp.exp(-x))


def _rms_rows(x, w, eps=1e-6):
    return x * lax.rsqrt(jnp.mean(x * x, axis=-1, keepdims=True) + eps) * w


def _params(n_axes):
    return pltpu.CompilerParams(
        dimension_semantics=("arbitrary",) * n_axes, vmem_limit_bytes=VMEM_LIMIT)


def _const_spec(shape):
    return pl.BlockSpec(shape, lambda *_: (0,) * len(shape), pipeline_mode=pl.Buffered(1))


def _front_or_frames(x_ref, meta_ref, is_front):
    tm, d = x_ref.shape
    front = jnp.concatenate([jnp.zeros((tm - N_META, d), F32), meta_ref[...]], axis=0)
    return jnp.where(is_front, front, x_ref[...])


def _in_proj_kernel(x_ref, meta_ref, nw_ref, w_ref, qt_ref, k_ref, vt_ref, u_ref):
    d = D_MODEL
    h = _front_or_frames(x_ref, meta_ref, pl.program_id(1) == 0)
    n = _rms_rows(h, nw_ref[...]).astype(BF16)
    q = _dot(n, w_ref[:, 0:d]) * (HEAD_DIM ** -0.5 * LOG2_E)
    for t in range(qt_ref.shape[0]):
        qt_ref[t] = q[t * TQ:(t + 1) * TQ, :].T.astype(BF16)
    k_ref[...] = _dot(n, w_ref[:, d:2 * d]).astype(BF16)
    v = _dot(n, w_ref[:, 2 * d:3 * d])
    for t in range(vt_ref.shape[0]):
        vt_ref[t] = v[t * TK:(t + 1) * TK, :].T.astype(BF16)
    a = _dot(n, w_ref[:, 3 * d:4 * d])
    g = _dot(n, w_ref[:, 4 * d:5 * d])
    u_ref[...] = a * _sigmoid(g)


def _in_proj(x, meta, norm_w, w_in):
    b, s, d = x.shape
    l = s + FRONT
    tm = TM_IN
    row = pl.BlockSpec((None, tm, d), lambda bi, i: (bi, i, 0))
    frames = pl.BlockSpec((None, tm, d), lambda bi, i: (bi, jnp.maximum(i - 1, 0), 0))
    qt = pl.BlockSpec((None, tm // TQ, d, TQ), lambda bi, i: (bi, i, 0, 0))
    vt = pl.BlockSpec((None, tm // TK, d, TK), lambda bi, i: (bi, i, 0, 0))
    return pl.pallas_call(
        _in_proj_kernel,
        grid=(b, l // tm),
        in_specs=[frames, _const_spec(meta.shape), _const_spec((1, d)), _const_spec((d, 5 * d))],
        out_specs=[qt, row, vt, row],
        out_shape=[jax.ShapeDtypeStruct((b, l // TQ, d, TQ), BF16),
                   jax.ShapeDtypeStruct((b, l, d), BF16),
                   jax.ShapeDtypeStruct((b, l // TK, d, TK), BF16),
                   jax.ShapeDtypeStruct((b, l, d), F32)],
        compiler_params=_params(2),
        name="in_proj",
    )(x, meta, norm_w, w_in)


_PLAIN, _DIAG, _FRONT_KEYS, _DIAG_AND_FRONT = 0, 1, 2, 3


def _attn_steps(n_qt):
    return sum(i // 2 + i % 2 for i in range(1, n_qt))


def _attn_kernel(qt_ref, k_ref, vt_ref, lq1_ref, lk1_ref, lq2_ref, lk2_ref, o_ref,
                 tbl_ref, onehot_ref, m_ref, acc_ref, *slot_refs, n_qt):
    tq, tk, w = TQ, TK, ATTN_W
    n_steps = _attn_steps(n_qt)
    n_slots = 2 * w
    slots = tuple(zip(slot_refs[0:n_slots], slot_refs[n_slots:2 * n_slots], slot_refs[2 * n_slots:]))

    key = lax.broadcasted_iota(jnp.int32, (tk, V_HEAD_DIM), 0)
    lane = lax.broadcasted_iota(jnp.int32, (tk, V_HEAD_DIM), 1)
    chunk_lane = lane < tk // CHUNK

    def bias(masked):
        return jnp.where(chunk_lane & masked, NEG_INF, 0.0).astype(BF16)

    tbl_ref[_PLAIN] = jnp.zeros((tk, V_HEAD_DIM), BF16)
    tbl_ref[_DIAG] = bias(key // CHUNK > lane)
    tbl_ref[_FRONT_KEYS] = bias(key < tk - N_META)
    tbl_ref[_DIAG_AND_FRONT] = bias(
        ((key < tk // 2) & (key // CHUNK > lane)) | ((key >= tk // 2) & (key < tk - N_META)))

    dim = lax.broadcasted_iota(jnp.int32, (V_HEAD_DIM, tq), 0)
    qry = lax.broadcasted_iota(jnp.int32, (V_HEAD_DIM, tq), 1)
    for part in range(tk // tq):
        chunk = part * (tq // CHUNK) + qry // CHUNK
        onehot_ref[part] = jnp.where(dim == chunk, 1.0, 0.0).astype(BF16)

    o_ref[0] = jnp.zeros((V_HEAD_DIM, tq), BF16)
    acc_ref[...] = jnp.zeros(acc_ref.shape, F32)
    m_ref[...] = jnp.full(m_ref.shape, NEG_INF, F32)
    ones_rows = jnp.ones((DENOM_ROWS, tk), BF16)

    def stacked_queries(i):
        qt = qt_ref[jnp.minimum(i, n_qt - 1)]
        zeros = jnp.zeros((HEAD_DIM, tq), BF16)
        onehot = onehot_ref[i & (tk // tq - 1)]
        return jnp.concatenate([
            jnp.concatenate([qt[0:HEAD_DIM], zeros], axis=1),
            jnp.concatenate([zeros, qt[HEAD_DIM:]], axis=1),
            jnp.concatenate([onehot, onehot], axis=1)], axis=0)

    half = tk // 2

    def merged(i, t):
        return jnp.logical_and(i & 1 == 0, t == i >> 1)

    def score_stage(i, t, slot):
        s_ref, mx_ref, a_ref = slots[slot]
        diag_front = merged(i, t)
        lo = pl.multiple_of(t * tk, tk)
        hi = pl.multiple_of(jnp.where(diag_front, half, t * tk + half), half)
        kt = jnp.concatenate([k_ref[pl.ds(lo, half), :], k_ref[pl.ds(hi, half), :]], axis=0)
        kind = jnp.where(diag_front, _DIAG_AND_FRONT,
                         jnp.where(t == 0, _FRONT_KEYS, jnp.where(t == i >> 1, _DIAG, _PLAIN)))
        s = _dot(jnp.concatenate([kt, tbl_ref[kind]], axis=1), stacked_queries(i))
        s_ref[...] = s
        first = t == (i & 1) ^ 1
        m_prev = jnp.where(first, NEG_INF, m_ref[...])
        m_new = jnp.maximum(m_prev, jnp.max(s, axis=0, keepdims=True))
        mx_ref[...] = m_new
        a_ref[...] = jnp.exp2(m_prev - m_new)
        m_ref[...] = m_new

    def softmax_value_stage(i, t, slot):
        s_ref, mx_ref, a_ref = slots[slot]
        p = jnp.exp2(s_ref[...] - mx_ref[...]).astype(BF16)
        t_hi = jnp.where(merged(i, t), 0, t)
        v_aug = jnp.concatenate(
            [jnp.concatenate([vt_ref[t][:, 0:half], vt_ref[t_hi][:, half:]], axis=1), ones_rows],
            axis=0)
        par = i & (ATTN_TILE_BUFS - 1)
        acc_ref[par] = a_ref[...] * acc_ref[par] + _dot(v_aug, p)

    lam = (jnp.exp(jnp.sum(lq1_ref[...] * lk1_ref[...], axis=-1, keepdims=True))
           - jnp.exp(jnp.sum(lq2_ref[...] * lk2_ref[...], axis=-1, keepdims=True))
           + LAMBDA_INIT)

    def finalize(i):
        par = i & (ATTN_TILE_BUFS - 1)
        l = acc_ref[par, V_HEAD_DIM:V_HEAD_DIM + 1, :]
        acc = acc_ref[par, 0:V_HEAD_DIM, :]
        o = acc[:, 0:tq] / l[:, 0:tq] - lam * (acc[:, tq:] / l[:, tq:])
        o_ref[i] = (o * lax.rsqrt(jnp.mean(o * o, axis=0, keepdims=True) + 1e-6)).astype(BF16)

    def advance(i, t):
        wrap = t == i >> 1
        return jnp.where(wrap, i + 1, i), jnp.where(wrap, i & 1, t + 1)

    def finalize_tiles(lo, hi):
        lax.fori_loop(lo, hi, lambda i, carry: (finalize(i), carry)[1], 0)

    def following(step, count):
        steps = [step]
        for _ in range(count - 1):
            steps.append(advance(*steps[-1]))
        return steps[:count]

    def block(first, base):
        steps = following(first, 2 * w)
        for j in range(w):
            score_stage(*steps[w + j], (base + w + j) % n_slots)
            softmax_value_stage(*steps[j], base + j)
        finalize_tiles(steps[0][0], steps[w][0])
        return steps[w]

    start = (jnp.int32(1), jnp.int32(0))
    for j, step in enumerate(following(start, w)):
        score_stage(*step, j)
    n_trips = n_steps // n_slots
    first = lax.fori_loop(0, n_trips, lambda _, c: block(block(c, 0), w), start)
    for j, step in enumerate(following(first, n_steps - n_trips * n_slots)):
        if j >= w:
            score_stage(*step, j)
        softmax_value_stage(*step, j)
    finalize_tiles(first[0], n_qt)


def _attention(qt, k, vt, lq1, lk1, lq2, lk2):
    b, l, d = k.shape
    n_qt = l // TQ
    col = pl.BlockSpec((None, l, V_HEAD_DIM), lambda bi, h: (bi, 0, h))
    qtspec = pl.BlockSpec((None, n_qt, V_HEAD_DIM, TQ), lambda bi, h: (bi, 0, h, 0))
    vtspec = pl.BlockSpec((None, l // TK, V_HEAD_DIM, TK), lambda bi, h: (bi, 0, h, 0))
    lspec = _const_spec((1, HEAD_DIM))
    nq = 2 * TQ
    per_slot = lambda shape, dtype: [pltpu.VMEM(shape, dtype)] * (2 * ATTN_W)
    return pl.pallas_call(
        functools.partial(_attn_kernel, n_qt=n_qt),
        grid=(b, N_HEADS),
        in_specs=[qtspec, col, vtspec, lspec, lspec, lspec, lspec],
        out_specs=pl.BlockSpec((None, None, n_qt, V_HEAD_DIM, TQ), lambda bi, h: (bi, h, 0, 0, 0)),
        out_shape=jax.ShapeDtypeStruct((b, N_HEADS, n_qt, V_HEAD_DIM, TQ), BF16),
        scratch_shapes=[
            pltpu.VMEM((4, TK, V_HEAD_DIM), BF16),
            pltpu.VMEM((TK // TQ, V_HEAD_DIM, TQ), BF16),
            pltpu.VMEM((1, nq), F32),
            pltpu.VMEM((ATTN_TILE_BUFS, V_HEAD_DIM + DENOM_ROWS, nq), F32),
            *per_slot((TK, nq), F32),
            *per_slot((1, nq), F32),
            *per_slot((1, nq), F32),
        ],
        compiler_params=_params(2),
        name="diff_attn",
    )(qt, k, vt, lq1, lk1, lq2, lk2)


def _mix_kernel(u_ref, uh_ref, x_ref, meta_ref, yat_ref, sw_ref, cw_ref, cb_ref, lnw_ref,
                lnb_ref, wco_ref, nw_ref, wga_ref, wgc_ref, wo_ref, o_ref, ubuf_ref, act_ref):
    tm, d = act_ref.shape

    ubuf_ref[0:HALO, :] = uh_ref[...]
    ubuf_ref[HALO:, :] = u_ref[...]

    h = _front_or_frames(x_ref, meta_ref, pl.program_id(1) == 0)
    n = _rms_rows(h, nw_ref[...]).astype(BF16)
    y_att = jnp.concatenate([yat_ref[hd].astype(F32).T for hd in range(N_HEADS)], axis=1)
    gate_att = _sigmoid(_dot(n, wga_ref[...])) * (y_att * sw_ref[...])
    gate_conv = _sigmoid(_dot(n, wgc_ref[...]))

    for base in range(0, tm, CONV_ROWS):
        win = ubuf_ref[base:base + CONV_ROWS + HALO, :]
        groups = (CONV_ROWS // SUBLANES, SUBLANES, d)
        acc = jnp.broadcast_to(cb_ref[...][None], groups)
        for c in range(SUBLANES):
            taps = [t for t in range(CONV_K) if (HALO - CONV_K + 1 + t) % SUBLANES == c]
            span = max(HALO - CONV_K + 1 + t for t in taps) - c + CONV_ROWS
            shifted = win[c:c + span, :]
            for t in taps:
                q = HALO - CONV_K + 1 + t - c
                acc = acc + cw_ref[t][None] * shifted[q:q + CONV_ROWS, :].reshape(groups)
        acc = acc.reshape(CONV_ROWS, d)
        mu = jnp.mean(acc, axis=-1, keepdims=True)
        xc = acc - mu
        y = xc * lax.rsqrt(jnp.mean(xc * xc, axis=-1, keepdims=True) + 1e-5)
        y = y * lnw_ref[...] + lnb_ref[...]
        act_ref[base:base + CONV_ROWS, :] = (y * _sigmoid(y)).astype(BF16)

    m = gate_att + gate_conv * _dot(act_ref[...], wco_ref[...])
    o_ref[...] = h + _dot(m.astype(BF16), wo_ref[...])


def _mix(u, x, meta, y_att_t, subln_w, conv_w, conv_b, ln_w, ln_b, w_conv_out, norm_w, w_in,
         w_out):
    b, l, d = u.shape
    tm = TM_ROW
    skip = SKIP // tm
    row = pl.BlockSpec((None, tm, d), lambda bi, i: (bi, i + skip, 0))
    frames = pl.BlockSpec((None, tm, d), lambda bi, i: (bi, jnp.maximum(i - 1, 0), 0))
    halo = pl.BlockSpec((None, HALO, d), lambda bi, i: (bi, (i + skip) * (tm // HALO) - 1, 0))
    vec = _const_spec((1, d))
    assert tm == TQ
    heads = pl.BlockSpec((None, N_HEADS, None, V_HEAD_DIM, TQ), lambda bi, i: (bi, 0, i + skip, 0, 0))
    return pl.pallas_call(
        _mix_kernel,
        grid=(b, l // tm - skip),
        in_specs=[row, halo, frames, _const_spec(meta.shape), heads, vec, _const_spec(conv_w.shape),
                  _const_spec(conv_b.shape), vec, vec,
                  _const_spec(w_conv_out.shape), vec,
                  pl.BlockSpec((d, d), lambda *_: (0, 5)), pl.BlockSpec((d, d), lambda *_: (0, 6)),
                  _const_spec(w_out.shape)],
        out_specs=pl.BlockSpec((None, tm, d), lambda bi, i: (bi, i, 0)),
        out_shape=jax.ShapeDtypeStruct((b, l - SKIP, d), F32),
        scratch_shapes=[pltpu.VMEM((tm + HALO, d), F32), pltpu.VMEM((tm, d), BF16)],
        compiler_params=_params(2),
        name="conv_mix",
    )(u, u, x, meta, y_att_t, subln_w, conv_w, conv_b, ln_w, ln_b, w_conv_out, norm_w, w_in,
      w_in, w_out)


def _ffn_kernel(h_ref, nw_ref, wup_ref, fw_ref, fb_ref, wdn_ref, fnw_ref, o_ref,
                up_ref, act_ref):
    i = pl.program_id(1)
    tm = h_ref.shape[0]
    ff = D_FF
    lc = 256

    @pl.when(i == 0)
    def _():
        up_ref[0:FFN_HALO, :] = jnp.zeros((FFN_HALO, 2 * ff), F32)

    h = h_ref[...]
    n = _rms_rows(h, nw_ref[...]).astype(BF16)
    up_ref[FFN_HALO:, :] = _dot(n, wup_ref[...])

    def conv(c0):
        acc = jnp.broadcast_to(fb_ref[:, c0:c0 + lc], (tm, lc))
        for t in range(FFN_CONV_K):
            r0 = FFN_HALO - (FFN_CONV_K - 1) + t
            acc = acc + fw_ref[t:t + 1, c0:c0 + lc] * up_ref[r0:r0 + tm, c0:c0 + lc]
        return acc

    for c in range(ff // lc):
        gate = conv(c * lc)
        val = conv(ff + c * lc)
        act_ref[:, c * lc:(c + 1) * lc] = (gate * _sigmoid(gate) * val).astype(BF16)

    up_ref[0:FFN_HALO, :] = up_ref[tm:tm + FFN_HALO, :]
    h3 = h + _dot(act_ref[...], wdn_ref[...])
    o_ref[...] = _rms_rows(h3, fnw_ref[...])


def _ffn(h2, norm_w, w_up, ffn_w, ffn_b, w_down, final_w, seq):
    b, rows, d = h2.shape
    tm = TM_ROW
    front_tiles = (FRONT - SKIP) // tm
    vec = _const_spec((1, d))
    return pl.pallas_call(
        _ffn_kernel,
        grid=(b, rows // tm),
        in_specs=[pl.BlockSpec((None, tm, d), lambda bi, i: (bi, i, 0)), vec,
                  _const_spec(w_up.shape), _const_spec(ffn_w.shape),
                  _const_spec(ffn_b.shape), _const_spec(w_down.shape), vec],
        out_specs=pl.BlockSpec((None, tm, d),
                               lambda bi, i: (bi, jnp.maximum(i - front_tiles, 0), 0)),
        out_shape=jax.ShapeDtypeStruct((b, seq, d), F32),
        scratch_shapes=[pltpu.VMEM((tm + FFN_HALO, 2 * D_FF), F32),
                        pltpu.VMEM((tm, D_FF), BF16)],
        compiler_params=_params(2),
        name="ffn",
    )(h2, norm_w, w_up, ffn_w, ffn_b, w_down, final_w)


def kernel(x, meta_tokens, norm_mix_w, w_in, lambda_q1, lambda_k1, lambda_q2, lambda_k2,
           subln_w, conv_dw_w, conv_dw_b, conv_ln_w, conv_ln_b, w_conv_out, w_out,
           norm_ffn_w, w_up, ffn_dw_w, ffn_dw_b, w_down, norm_final_w):
    b, s, d = x.shape
    assert d == D_MODEL and w_in.shape[0] == 1, "single-layer trunk"
    assert s % TK == 0 and FRONT == TK == 2 * TQ and SKIP == TM_ROW and FRONT - SKIP == TM_ROW
    assert SKIP + HALO <= FRONT - N_META and TK // CHUNK <= V_HEAD_DIM and FRONT == TM_IN
    meta = meta_tokens.astype(F32)

    w_in_bf = w_in[0].astype(BF16)
    vec = lambda a: a.reshape(1, -1).astype(F32)

    qt, k, vt, u = _in_proj(x, meta, vec(norm_mix_w[0]), w_in_bf)
    y_att_t = _attention(qt, k, vt, vec(lambda_q1[0]), vec(lambda_k1[0]), vec(lambda_q2[0]),
                         vec(lambda_k2[0]))
    tile_rows = lambda a: jnp.broadcast_to(a.astype(F32)[..., None, :], a.shape[:-1] + (SUBLANES, d))
    subln_gain = vec(jnp.tile(subln_w[0], N_HEADS)) * (1.0 - LAMBDA_INIT)
    h2 = _mix(u, x, meta, y_att_t, subln_gain, tile_rows(conv_dw_w[0]),
              tile_rows(conv_dw_b[0]), vec(conv_ln_w[0]), vec(conv_ln_b[0]),
              w_conv_out[0].astype(BF16), vec(norm_mix_w[0]), w_in_bf, w_out[0].astype(BF16))
    return _ffn(h2, vec(norm_ffn_w[0]), w_up[0].astype(BF16), ffn_dw_w[0].astype(F32),
                vec(ffn_dw_b[0]), w_down[0].astype(BF16), vec(norm_final_w), s)
```
